```python
import math
import jax
import jax.numpy as jnp
from jax import lax
import numpy as np

D_MODEL = 4096
BATCH = 2
SEQ = 8192
DEPTH = 4

CTX_LEN = 256
GRID_W = 64
HEAD_DIM = 128
CONV_CH = D_MODEL // 4
CONV_W = 31
NA_W = 3 * D_MODEL // 8
NA_HEADS = NA_W // HEAD_DIM
WIN_R = 8
WIN_C = 16
DIFF_W = 3 * D_MODEL // 8
DIFF_HEADS = DIFF_W // HEAD_DIM
DIFF_DIM = HEAD_DIM // 2
ROPE_BASE = 10000.0
MIX_W = CONV_CH + NA_W + DIFF_W
COL_SIZES = (NA_W, NA_W, DIFF_W, DIFF_W, NA_W, DIFF_W, CONV_CH, CONV_CH)
KV_COLS = 2 * NA_W + 2 * DIFF_W
IN_COLS = KV_COLS + NA_W + DIFF_W + 2 * CONV_CH
N_EXPERTS = 16
EC_CAPACITY = 2
EXPERT_FF = D_MODEL // 8
ADA_RANK = D_MODEL // 16
N_MOD = 6
Q_BLOCK = 128
EPS = 1e-6

kernel_name = 'hybrid_parallel_group_diffusion_block'


def _rms(x):
    xf = x.astype(jnp.float32)
    return (xf * lax.rsqrt(jnp.mean(xf * xf, axis=-1, keepdims=True) + EPS)).astype(x.dtype)


def _layernorm(x, g, b):
    xf = x.astype(jnp.float32)
    mu = jnp.mean(xf, axis=-1, keepdims=True)
    var = jnp.mean(jnp.square(xf - mu), axis=-1, keepdims=True)
    return ((xf - mu) * lax.rsqrt(var + EPS)).astype(x.dtype) * g + b


def _ada(cvec, down, up, bias, n):
    z = jax.nn.silu(cvec) @ down
    m = z @ up[:, :n * D_MODEL] + bias[:n * D_MODEL]
    return m.reshape(cvec.shape[:-1] + (n, D_MODEL))


def _split(p, sizes):
    cuts = [int(s) for s in np.cumsum(sizes)[:-1]]
    return jnp.split(p, cuts, axis=-1)


def _conformer_conv(a_val, a_gate, w_dw, b_dw, ln_g, ln_b, w_pw):
    u = a_val * jax.nn.sigmoid(a_gate)
    u = lax.conv_general_dilated(
        u, w_dw[:, None, :].astype(u.dtype), window_strides=(1,),
        padding=[(CONV_W // 2, CONV_W // 2)],
        dimension_numbers=('NWC', 'WIO', 'NWC'),
        feature_group_count=u.shape[-1]) + b_dw
    u = _layernorm(u, ln_g, ln_b)
    return jax.nn.silu(u) @ w_pw


def _dense_attend(q, k, v):
    s = jnp.einsum('bqhe,bkhe->bhqk', q, k).astype(jnp.float32) * (q.shape[-1] ** -0.5)
    p = jax.nn.softmax(s, axis=-1).astype(v.dtype)
    return jnp.einsum('bhqk,bkhe->bqhe', p, v)


def _na_latent(q, k, v, k_ctx, v_ctx, rpb):
    B, T, H, hd = q.shape
    rows = T // GRID_W
    wr = min(WIN_R, rows)
    nqb = GRID_W // WIN_C
    kw = 2 * WIN_C
    qcol = np.arange(GRID_W).reshape(nqb, WIN_C)
    kc0 = np.clip(np.arange(nqb) * WIN_C - WIN_C // 2, 0, GRID_W - kw)
    kcol = kc0[:, None] + np.arange(kw)
    cs = np.clip(qcol - WIN_C // 2, 0, GRID_W - WIN_C)
    col_ok = (kcol[:, None, :] >= cs[:, :, None]) & (kcol[:, None, :] < cs[:, :, None] + WIN_C)
    dc_idx = np.clip(kcol[:, None, :] - qcol[:, :, None] + WIN_C - 1, 0, 2 * WIN_C - 2)
    col_ok = jnp.asarray(col_ok)[:, :, None, :]
    rpb_c = rpb.astype(jnp.float32)[:, :, dc_idx]
    qg = q.reshape(B, rows, nqb, WIN_C, H, hd)
    kg = k.reshape(B, rows, GRID_W, H, hd)
    vg = v.reshape(B, rows, GRID_W, H, hd)
    scale = hd ** -0.5

    def row_fn(r):
        rs = jnp.clip(r - wr // 2, 0, rows - wr)
        ks = lax.dynamic_slice_in_dim(kg, rs, wr, axis=1)[:, :, kcol]
        vs = lax.dynamic_slice_in_dim(vg, rs, wr, axis=1)[:, :, kcol]
        qr = lax.dynamic_index_in_dim(qg, r, axis=1, keepdims=False)
        s_loc = jnp.einsum('bjqhd,brjkhd->bhjqrk', qr, ks).astype(jnp.float32) * scale
        dr_idx = rs + jnp.arange(wr) - r + WIN_R - 1
        bias = jnp.take(rpb_c, dr_idx, axis=1).transpose(0, 2, 3, 1, 4)
        s_loc = jnp.where(col_ok, s_loc + bias, -jnp.inf)
        s_ctx = jnp.einsum('bjqhd,bkhd->bhjqk', qr, k_ctx).astype(jnp.float32) * scale
        s = jnp.concatenate([s_loc.reshape(B, H, nqb, WIN_C, wr * kw), s_ctx], axis=-1)
        p = jax.nn.softmax(s, axis=-1).astype(v.dtype)
        p_loc = p[..., :wr * kw].reshape(B, H, nqb, WIN_C, wr, kw)
        p_ctx = p[..., wr * kw:]
        o = (jnp.einsum('bhjqrk,brjkhd->bjqhd', p_loc, vs)
             + jnp.einsum('bhjqk,bkhd->bjqhd', p_ctx, v_ctx))
        return o.reshape(B, GRID_W, H * hd)

    o = lax.map(row_fn, jnp.arange(rows))
    return jnp.moveaxis(o, 0, 1).reshape(B, T, H * hd)


def _rope_axial(t, cos_r, sin_r, cos_c, sin_c):
    tf = t.astype(jnp.float32)
    half = t.shape[-1] // 2

    def rot(u, cos, sin):
        u1, u2 = jnp.split(u, 2, axis=-1)
        cos = cos[None, :, None, None, :]
        sin = sin[None, :, None, None, :]
        return jnp.concatenate([u1 * cos - u2 * sin, u2 * cos + u1 * sin], axis=-1)

    out = jnp.concatenate([rot(tf[..., :half], cos_r, sin_r), rot(tf[..., half:], cos_c, sin_c)], axis=-1)
    return out.astype(t.dtype)


def _diff_attend(q, k, v, lam):
    s = jnp.einsum('bqhcd,bkhcd->bhcqk', q, k).astype(jnp.float32) * (q.shape[-1] ** -0.5)
    p = jax.nn.softmax(s, axis=-1)
    a = (p[:, :, 0] - lam * p[:, :, 1]).astype(v.dtype)
    return jnp.einsum('bhqk,bkhe->bqhe', a, v)


def _diff_latent(q, k, v, k_ctx, v_ctx, lam):
    B, T = q.shape[:2]
    kk = jnp.concatenate([k_ctx, k], axis=1)
    vv = jnp.concatenate([v_ctx, v], axis=1)
    nb = T // Q_BLOCK
    qb = jnp.moveaxis(q.reshape((B, nb, Q_BLOCK) + q.shape[2:]), 1, 0)
    o = lax.map(lambda qi: _diff_attend(qi, kk, vv, lam), qb)
    return jnp.moveaxis(o, 0, 1).reshape((B, T) + v.shape[2:])


def _diff_post(o, g, lam_init):
    B, Q = o.shape[:2]
    return (_rms(o) * g * (1.0 - lam_init)).reshape(B, Q, -1)


def _ec_moe(h, w_router, w_gate, w_up, w_down):
    B, L, D = h.shape
    cap = max(1, EC_CAPACITY * L // N_EXPERTS)
    aff = jax.nn.softmax(jnp.einsum('bld,de->ble', h, w_router).astype(jnp.float32), axis=-1)
    g, idx = lax.top_k(jnp.swapaxes(aff, 1, 2), cap)
    bidx = jnp.arange(B)[:, None, None]
    xs = h[bidx, idx]
    a = jnp.einsum('becd,edf->becf', xs, w_gate)
    u = jnp.einsum('becd,edf->becf', xs, w_up)
    y = jnp.einsum('becf,efd->becd', jax.nn.silu(a) * u, w_down) * g[..., None].astype(h.dtype)
    return jnp.zeros_like(h).at[bidx, idx].add(y)


def setup_inputs(seed: int = 0) -> dict:
    key = jax.random.key(seed)
    ks = jax.random.split(key, 26)
    f32 = jnp.float32
    L, D = DEPTH, D_MODEL

    def nrm(k, shape, s):
        return jax.random.normal(k, shape, f32) * s

    return {
        'x': nrm(ks[0], (BATCH, SEQ, D), 1.0),
        'c': nrm(ks[1], (BATCH, D), 1.0),
        'ctx': nrm(ks[2], (BATCH, CTX_LEN, D), 1.0),
        'c_ctx': nrm(ks[3], (D,), 1.0),
        'ada_down': nrm(ks[4], (L, D, ADA_RANK), D ** -0.5),
        'ada_up': nrm(ks[5], (L, ADA_RANK, N_MOD * D), 0.3 * ADA_RANK ** -0.5),
        'ada_bias': nrm(ks[6], (L, N_MOD * D), 0.02),
        'w_in': nrm(ks[7], (L, D, IN_COLS), D ** -0.5),
        'conv_dw': nrm(ks[8], (L, CONV_W, CONV_CH), CONV_W ** -0.5),
        'conv_db': nrm(ks[9], (L, CONV_CH), 0.02),
        'conv_ln_g': 1.0 + nrm(ks[10], (L, CONV_CH), 0.02),
        'conv_ln_b': nrm(ks[11], (L, CONV_CH), 0.02),
        'conv_pw': nrm(ks[12], (L, CONV_CH, CONV_CH), CONV_CH ** -0.5),
        'na_q_gain': 1.0 + nrm(ks[13], (L, HEAD_DIM), 0.02),
        'na_k_gain': 1.0 + nrm(ks[14], (L, HEAD_DIM), 0.02),
        'na_rpb': nrm(ks[15], (L, NA_HEADS, 2 * WIN_R - 1, 2 * WIN_C - 1), 0.1),
        'diff_q_gain': 1.0 + nrm(ks[16], (L, 2, DIFF_DIM), 0.02),
        'diff_k_gain': 1.0 + nrm(ks[17], (L, 2, DIFF_DIM), 0.02),
        'diff_lam': nrm(ks[18], (L, 4, DIFF_DIM), 0.1),
        'diff_out_gain': 1.0 + nrm(ks[19], (L, HEAD_DIM), 0.02),
        'w_out': nrm(ks[20], (L, MIX_W, D), MIX_W ** -0.5),
        'w_router': nrm(ks[21], (L, D, N_EXPERTS), D ** -0.5),
        'w_gate': nrm(ks[22], (L, N_EXPERTS, D, EXPERT_FF), D ** -0.5),
        'w_up': nrm(ks[23], (L, N_EXPERTS, D, EXPERT_FF), D ** -0.5),
        'w_down': nrm(ks[24], (L, N_EXPERTS, EXPERT_FF, D), EXPERT_FF ** -0.5),
    }


def reference(x, c, ctx, c_ctx, ada_down, ada_up, ada_bias, w_in, conv_dw, conv_db,
              conv_ln_g, conv_ln_b, conv_pw, na_q_gain, na_k_gain, na_rpb, diff_q_gain,
              diff_k_gain, diff_lam, diff_out_gain, w_out, w_router, w_gate, w_up, w_down):
    B, T, _ = x.shape
    Lc = ctx.shape[1]
    nf = DIFF_DIM // 4
    inv = ROPE_BASE ** (-jnp.arange(nf, dtype=jnp.float32) / nf)
    t_idx = jnp.arange(T)
    ang_r = (t_idx // GRID_W).astype(jnp.float32)[:, None] * inv
    ang_c = (t_idx % GRID_W).astype(jnp.float32)[:, None] * inv
    cos_r, sin_r, cos_c, sin_c = jnp.cos(ang_r), jnp.sin(ang_r), jnp.cos(ang_c), jnp.sin(ang_c)

    for l in range(DEPTH):
        update_ctx = l < DEPTH - 1
        lam_init = 0.8 - 0.6 * math.exp(-0.3 * l)
        lv = diff_lam[l].astype(jnp.float32)
        lam = jnp.exp(jnp.sum(lv[0] * lv[1])) - jnp.exp(jnp.sum(lv[2] * lv[3])) + lam_init
        wl = w_in[l]
        m = _ada(c, ada_down[l], ada_up[l], ada_bias[l], N_MOD)
        mc = _ada(c_ctx, ada_down[l], ada_up[l], ada_bias[l], N_MOD if update_ctx else 2)

        hc = _rms(ctx) * (1 + mc[1]) + mc[0]
        kna_c, vna_c, kdf_c, vdf_c = _split(hc @ wl[:, :KV_COLS], COL_SIZES[:4])
        kna_c = _rms(kna_c.reshape(B, Lc, NA_HEADS, HEAD_DIM)) * na_k_gain[l]
        vna_c = vna_c.reshape(B, Lc, NA_HEADS, HEAD_DIM)
        kdf_c = _rms(kdf_c.reshape(B, Lc, DIFF_HEADS, 2, DIFF_DIM)) * diff_k_gain[l]
        vdf_c = vdf_c.reshape(B, Lc, DIFF_HEADS, HEAD_DIM)

        h = _rms(x) * (1 + m[:, 1, None]) + m[:, 0, None]
        k_na, v_na, k_df, v_df, q_na, q_df, a_val, a_gate = _split(h @ wl, COL_SIZES)
        q_na = _rms(q_na.reshape(B, T, NA_HEADS, HEAD_DIM)) * na_q_gain[l]
        k_na = _rms(k_na.reshape(B, T, NA_HEADS, HEAD_DIM)) * na_k_gain[l]
        v_na = v_na.reshape(B, T, NA_HEADS, HEAD_DIM)
        o_na = _na_latent(q_na, k_na, v_na, kna_c, vna_c, na_rpb[l])
        q_df = _rope_axial(_rms(q_df.reshape(B, T, DIFF_HEADS, 2, DIFF_DIM)) * diff_q_gain[l],
                           cos_r, sin_r, cos_c, sin_c)
        k_df = _rope_axial(_rms(k_df.reshape(B, T, DIFF_HEADS, 2, DIFF_DIM)) * diff_k_gain[l],
                           cos_r, sin_r, cos_c, sin_c)
        v_df = v_df.reshape(B, T, DIFF_HEADS, HEAD_DIM)
        o_df = _diff_post(_diff_latent(q_df, k_df, v_df, kdf_c, vdf_c, lam),
                          diff_out_gain[l], lam_init)
        o_cv = _conformer_conv(a_val, a_gate, conv_dw[l], conv_db[l], conv_ln_g[l],
                               conv_ln_b[l], conv_pw[l])
        mix = jnp.concatenate([o_cv, o_na, o_df], axis=-1) @ w_out[l]
        x = x + m[:, 2, None] * mix
        h2 = _rms(x) * (1 + m[:, 4, None]) + m[:, 3, None]
        x = x + m[:, 5, None] * _ec_moe(h2, w_router[l], w_gate[l], w_up[l], w_down[l])

        if update_ctx:
            qna_c, qdf_c, av_c, ag_c = _split(hc @ wl[:, KV_COLS:], COL_SIZES[4:])
            qna_c = _rms(qna_c.reshape(B, Lc, NA_HEADS, HEAD_DIM)) * na_q_gain[l]
            o_na_c = _dense_attend(qna_c, kna_c, vna_c).reshape(B, Lc, NA_W)
            qdf_c = _rms(qdf_c.reshape(B, Lc, DIFF_HEADS, 2, DIFF_DIM)) * diff_q_gain[l]
            o_df_c = _diff_post(_diff_attend(qdf_c, kdf_c, vdf_c, lam), diff_out_gain[l], lam_init)
            o_cv_c = _conformer_conv(av_c, ag_c, conv_dw[l], conv_db[l], conv_ln_g[l],
                                     conv_ln_b[l], conv_pw[l])
            mix_c = jnp.concatenate([o_cv_c, o_na_c, o_df_c], axis=-1) @ w_out[l]
            ctx = ctx + mc[2] * mix_c
            h2c = _rms(ctx) * (1 + mc[4]) + mc[3]
            ctx = ctx + mc[5] * _ec_moe(h2c, w_router[l], w_gate[l], w_up[l], w_down[l])
    return x
```

```python
import functools
import math

import numpy as np
import jax
import jax.numpy as jnp
from jax import lax
from jax.experimental import pallas as pl
from jax.experimental.pallas import tpu as pltpu

GRID_W = 64
HEAD_DIM = 128
WIN_R = 8
WIN_C = 16
DIFF_DIM = HEAD_DIM // 2
ROPE_BASE = 10000.0
EC_CAPACITY = 2
N_MOD = 6
EPS = 1e-6
LANES = 128
NA_QROWS = 4
NEG = -1e30
VMEM_LIMIT = 56 * 1024 * 1024

F32 = jnp.float32
BF16 = jnp.bfloat16


def _cparams(sem):
    return pltpu.CompilerParams(dimension_semantics=sem, vmem_limit_bytes=VMEM_LIMIT)


def _pick(n, cap, mult=LANES):
    best = None
    for t in range(mult, min(n, cap) + 1, mult):
        if n % t == 0:
            best = t
    assert best is not None, (n, cap, mult)
    return best


def _dot(a, b):
    return jnp.dot(a, b, preferred_element_type=F32)


def _dot_nt(a, b):
    return lax.dot_general(a, b, (((1,), (1,)), ((), ())), preferred_element_type=F32)


def _norm_kernel(x_ref, sc_ref, sh_ref, o_ref):
    x = x_ref[...]
    ms = jnp.mean(x * x, axis=-1, keepdims=True)
    h = x * lax.rsqrt(ms + EPS) * (1.0 + sc_ref[...]) + sh_ref[...]
    o_ref[...] = h.astype(o_ref.dtype)


def _norm_router_kernel(x_ref, sc_ref, sh_ref, wr_ref, o_ref, lg_ref):
    x = x_ref[...]
    ms = jnp.mean(x * x, axis=-1, keepdims=True)
    h = (x * lax.rsqrt(ms + EPS) * (1.0 + sc_ref[...]) + sh_ref[...]).astype(o_ref.dtype)
    o_ref[...] = h
    lg_ref[...] = _dot(h, wr_ref[...])


def _norm_mod(x, scale, shift, w_router=None):
    B, L, D = x.shape
    tl = _pick(L, 512, 8)
    grid = (B, L // tl)
    xspec = pl.BlockSpec((None, tl, D), lambda b, i: (b, i, 0))
    mspec = pl.BlockSpec((None, 1, D), lambda b, i: (b, 0, 0))
    if w_router is None:
        return pl.pallas_call(
            _norm_kernel, grid=grid, in_specs=[xspec, mspec, mspec], out_specs=xspec,
            out_shape=jax.ShapeDtypeStruct((B, L, D), BF16),
            compiler_params=_cparams(("parallel", "parallel")), name="norm_mod",
        )(x, scale, shift)
    E = w_router.shape[1]
    wr = jnp.zeros((D, LANES), BF16).at[:, :E].set(w_router.astype(BF16))
    return pl.pallas_call(
        _norm_router_kernel, grid=grid,
        in_specs=[xspec, mspec, mspec, pl.BlockSpec((D, LANES), lambda b, i: (0, 0))],
        out_specs=[xspec, pl.BlockSpec((None, tl, LANES), lambda b, i: (b, i, 0))],
        out_shape=[jax.ShapeDtypeStruct((B, L, D), BF16),
                   jax.ShapeDtypeStruct((B, L, LANES), F32)],
        compiler_params=_cparams(("parallel", "parallel")), name="norm_router",
    )(x, scale, shift, wr)


def _proj_kernel(kind, rope, tn, *refs):
    if kind == "glu":
        h_ref, w_ref, w2_ref, o_ref = refs
        h = h_ref[...]
        val = _dot(h, w_ref[...])
        gate = _dot(h, w2_ref[...])
        o_ref[...] = val * jax.nn.sigmoid(gate)
        return
    if kind == "plain":
        h_ref, w_ref, o_ref = refs
        o_ref[...] = _dot(h_ref[...], w_ref[...]).astype(o_ref.dtype)
        return
    if rope:
        h_ref, w_ref, g_ref, cos_ref, sin_ref, o_ref = refs
    else:
        h_ref, w_ref, g_ref, o_ref = refs
    acc = _dot(h_ref[...], w_ref[...])
    for s in range(tn // LANES):
        t = acc[:, s * LANES:(s + 1) * LANES]
        sq = t * t
        lane = lax.broadcasted_iota(jnp.int32, t.shape, 1)
        if kind == "rms128":
            inv = lax.rsqrt(jnp.mean(sq, axis=-1, keepdims=True) + EPS)
        else:
            lo = jnp.sum(jnp.where(lane < DIFF_DIM, sq, 0.0), axis=-1, keepdims=True)
            tot = jnp.sum(sq, axis=-1, keepdims=True)
            inv = jnp.where(lane < DIFF_DIM,
                            lax.rsqrt(lo * (1.0 / DIFF_DIM) + EPS),
                            lax.rsqrt((tot - lo) * (1.0 / DIFF_DIM) + EPS))
        t = t * inv * g_ref[...]
        if rope:
            up = pltpu.roll(t, LANES - 16, axis=1)
            dn = pltpu.roll(t, 16, axis=1)
            sw = jnp.where((lane % 32) < 16, up, dn)
            t = t * cos_ref[...] + sw * sin_ref[...]
        o_ref[:, s * LANES:(s + 1) * LANES] = t.astype(o_ref.dtype)


def _proj(h, w, col0, ncols, kind, *, gain=None, cos=None, sin=None, col0_b=None):
    M, D = h.shape
    tm = _pick(M, 1024, 8)
    tn = _pick(ncols, 512)
    assert col0 % tn == 0
    grid = (M // tm, ncols // tn)
    hspec = pl.BlockSpec((tm, D), lambda i, j: (i, 0))
    c0 = col0 // tn
    wspec = pl.BlockSpec((D, tn), lambda i, j: (0, c0 + j))
    ospec = pl.BlockSpec((tm, tn), lambda i, j: (i, j))
    rope = cos is not None
    ins, specs = [h, w], [hspec, wspec]
    odt = BF16
    if kind == "glu":
        assert col0_b % tn == 0
        c1 = col0_b // tn
        ins.append(w)
        specs.append(pl.BlockSpec((D, tn), lambda i, j: (0, c1 + j)))
        odt = F32
    elif kind != "plain":
        ins.append(gain.reshape(1, LANES).astype(F32))
        specs.append(pl.BlockSpec((1, LANES), lambda i, j: (0, 0)))
        if rope:
            nt = cos.shape[0] // tm
            tspec = pl.BlockSpec((tm, LANES), lambda i, j: (i % nt, 0))
            ins += [cos, sin]
            specs += [tspec, tspec]
    return pl.pallas_call(
        functools.partial(_proj_kernel, kind, rope, tn), grid=grid,
        in_specs=specs, out_specs=ospec,
        out_shape=jax.ShapeDtypeStruct((M, ncols), odt),
        compiler_params=_cparams(("parallel", "arbitrary")), name="proj_" + kind,
    )(*ins)


def _na_bias(rpb, rows):
    nq = NA_QROWS * GRID_W
    nk = 3 * nq
    qr = np.arange(nq) // GRID_W
    qc = np.arange(nq) % GRID_W
    kr = np.arange(nk) // GRID_W
    kc = np.arange(nk) % GRID_W
    cs = np.clip(qc - WIN_C // 2, 0, GRID_W - WIN_C)
    col_ok = (kc[None, :] >= cs[:, None]) & (kc[None, :] < cs[:, None] + WIN_C)
    dc = np.clip(kc[None, :] - qc[:, None] + WIN_C - 1, 0, 2 * WIN_C - 2)
    tabs = []
    for typ in range(3):
        if typ == 0:
            dr = kr[None, :] - qr[:, None]
            row_ok = np.broadcast_to(kr[None, :] < WIN_R, (nq, nk))
        elif typ == 1:
            dr = kr[None, :] - qr[:, None] - NA_QROWS
            row_ok = (dr >= -(WIN_R // 2)) & (dr < WIN_R // 2)
        else:
            dr = kr[None, :] - qr[:, None] - 2 * NA_QROWS
            row_ok = np.broadcast_to(kr[None, :] >= 3 * NA_QROWS - WIN_R, (nq, nk))
        ok = row_ok & col_ok
        dri = np.clip(dr + WIN_R - 1, 0, 2 * WIN_R - 2)
        b = rpb.astype(F32)[:, dri, dc]
        tabs.append(jnp.where(jnp.asarray(ok)[None], b, NEG))
    return jnp.stack(tabs)


def _na_kernel(scale, q_ref, k0_ref, k1_ref, k2_ref, v0_ref, v1_ref, v2_ref,
               kc_ref, vc_ref, b_ref, o_ref):
    q = q_ref[...]
    nq = q.shape[0]
    krefs = (k0_ref, k1_ref, k2_ref)
    vrefs = (v0_ref, v1_ref, v2_ref)
    s_loc = [_dot_nt(q, krefs[d][...]) * scale + b_ref[:, d * nq:(d + 1) * nq] for d in range(3)]
    s_ctx = _dot_nt(q, kc_ref[...]) * scale
    m = jnp.max(s_ctx, axis=-1, keepdims=True)
    for s in s_loc:
        m = jnp.maximum(m, jnp.max(s, axis=-1, keepdims=True))
    p_ctx = jnp.exp(s_ctx - m)
    l = jnp.sum(p_ctx, axis=-1, keepdims=True)
    o = _dot(p_ctx.astype(BF16), vc_ref[...])
    for d in range(3):
        p = jnp.exp(s_loc[d] - m)
        l = l + jnp.sum(p, axis=-1, keepdims=True)
        o = o + _dot(p.astype(BF16), vrefs[d][...])
    o_ref[...] = (o / l).astype(o_ref.dtype)


def _na_attn(q, k, v, kc, vc, bias):
    B, T, W = q.shape
    Lc = kc.shape[1]
    H = W // HEAD_DIM
    nq = NA_QROWS * GRID_W
    nblk = T // nq
    assert T % nq == 0 and nblk >= 4
    grid = (H, B, nblk)

    def kmap(d):
        return lambda h, b, i: (b, jnp.clip(i - 1, 0, nblk - 3) + d, h)

    qspec = pl.BlockSpec((None, nq, HEAD_DIM), lambda h, b, i: (b, i, h))
    kspecs = [pl.BlockSpec((None, nq, HEAD_DIM), kmap(d)) for d in range(3)]
    cspec = pl.BlockSpec((None, Lc, HEAD_DIM), lambda h, b, i: (b, 0, h))
    bspec = pl.BlockSpec(
        (None, None, nq, 3 * nq),
        lambda h, b, i: (jnp.where(i == 0, 0, jnp.where(i == nblk - 1, 2, 1)), h, 0, 0))
    return pl.pallas_call(
        functools.partial(_na_kernel, HEAD_DIM ** -0.5), grid=grid,
        in_specs=[qspec] + kspecs + kspecs + [cspec, cspec, bspec], out_specs=qspec,
        out_shape=jax.ShapeDtypeStruct((B, T, W), BF16),
        compiler_params=_cparams(("parallel", "parallel", "arbitrary")), name="na_attn",
    )(q, k, k, k, v, v, v, kc, vc, bias)


def _dense_kernel(scale, q_ref, k_ref, v_ref, o_ref):
    s = _dot_nt(q_ref[...], k_ref[...]) * scale
    m = jnp.max(s, axis=-1, keepdims=True)
    p = jnp.exp(s - m)
    l = jnp.sum(p, axis=-1, keepdims=True)
    o_ref[...] = (_dot(p.astype(BF16), v_ref[...]) / l).astype(o_ref.dtype)


def _dense_attn(q, k, v):
    B, L, W = q.shape
    H = W // HEAD_DIM
    spec = pl.BlockSpec((None, L, HEAD_DIM), lambda b, h: (b, 0, h))
    return pl.pallas_call(
        functools.partial(_dense_kernel, HEAD_DIM ** -0.5), grid=(B, H),
        in_specs=[spec, spec, spec], out_specs=spec,
        out_shape=jax.ShapeDtypeStruct((B, L, W), BF16),
        compiler_params=_cparams(("parallel", "parallel")), name="dense_attn",
    )(q, k, v)


def _diff_kernel(tk, n_lat, post_scale, lam_ref, q_ref, kc_ref, vc_ref, *refs):
    if n_lat:
        k_ref, v_ref, g_ref, o_ref = refs
    else:
        g_ref, o_ref = refs
    q = q_ref[...]
    tq = q.shape[0]
    lane = lax.broadcasted_iota(jnp.int32, q.shape, 1)
    zero = jnp.zeros_like(q)
    qscale = jnp.asarray(DIFF_DIM ** -0.5, q.dtype)
    qs = jnp.concatenate([jnp.where(lane < DIFF_DIM, q, zero),
                          jnp.where(lane >= DIFF_DIM, q, zero)], axis=0) * qscale

    def step(kb, vb, carry):
        m, l, acc = carry
        s = _dot_nt(qs, kb)
        m_new = jnp.maximum(m, jnp.max(s, axis=-1, keepdims=True))
        alpha = jnp.exp(m - m_new)
        p = jnp.exp(s - m_new)
        l = alpha * l + jnp.sum(p, axis=-1, keepdims=True)
        acc = alpha * acc + _dot(p.astype(BF16), vb)
        return m_new, l, acc

    carry = (jnp.full((2 * tq, 1), -jnp.inf, F32), jnp.zeros((2 * tq, 1), F32),
             jnp.zeros((2 * tq, HEAD_DIM), F32))
    carry = step(kc_ref[...], vc_ref[...], carry)
    if n_lat:
        def body(c, carry):
            off = pl.multiple_of(c * tk, tk)
            return step(k_ref[pl.ds(off, tk), :], v_ref[pl.ds(off, tk), :], carry)
        carry = lax.fori_loop(0, n_lat, body, carry)
    _, l, acc = carry
    o = acc / l
    o = o[:tq] - lam_ref[0, 0] * o[tq:]
    o = o * lax.rsqrt(jnp.mean(o * o, axis=-1, keepdims=True) + EPS)
    o_ref[...] = (o * g_ref[...] * post_scale).astype(o_ref.dtype)


def _diff_attn(q, kc, vc, k, v, lam, gain, post_scale):
    B, L, W = q.shape
    Lc = kc.shape[1]
    H = W // HEAD_DIM
    tq = _pick(L, 256, 8)
    grid = (B, H, L // tq)
    qspec = pl.BlockSpec((None, tq, HEAD_DIM), lambda b, h, i: (b, i, h))
    cspec = pl.BlockSpec((None, Lc, HEAD_DIM), lambda b, h, i: (b, 0, h))
    ins = [lam.reshape(1, 1).astype(F32), q, kc, vc]
    specs = [pl.BlockSpec(memory_space=pltpu.SMEM), qspec, cspec, cspec]
    tk, n_lat = 0, 0
    if k is not None:
        T = k.shape[1]
        tk = _pick(T, 512, 8)
        n_lat = T // tk
        lspec = pl.BlockSpec((None, T, HEAD_DIM), lambda b, h, i: (b, 0, h))
        ins += [k, v]
        specs += [lspec, lspec]
    ins.append(gain.reshape(1, HEAD_DIM).astype(F32))
    specs.append(pl.BlockSpec((1, HEAD_DIM), lambda b, h, i: (0, 0)))
    return pl.pallas_call(
        functools.partial(_diff_kernel, tk, n_lat, post_scale), grid=grid,
        in_specs=specs, out_specs=qspec,
        out_shape=jax.ShapeDtypeStruct((B, L, W), BF16),
        compiler_params=_cparams(("parallel", "parallel", "arbitrary")), name="diff_attn",
    )(*ins)


CONV_HALO = 16
CONV_RC = 32


def _conv_kernel(cw, nt, prev_ref, cur_ref, next_ref, dw_ref, db_ref, g_ref, b_ref,
                 pw_ref, o_ref, pad_ref, y_ref):
    i = pl.program_id(1)
    tt, ch = cur_ref.shape
    half = cw // 2
    pad_ref[0:CONV_HALO, :] = jnp.where(i == 0, 0.0, prev_ref[...])
    pad_ref[CONV_HALO:CONV_HALO + tt, :] = cur_ref[...]
    pad_ref[CONV_HALO + tt:, :] = jnp.where(i == nt - 1, 0.0, next_ref[...])
    base = CONV_HALO - half
    for r0 in range(0, tt, CONV_RC):
        for c0 in range(0, ch, LANES):
            acc = jnp.zeros((CONV_RC, LANES), F32)
            for j in range(cw):
                acc = acc + dw_ref[j:j + 1, c0:c0 + LANES] * \
                    pad_ref[r0 + base + j:r0 + base + j + CONV_RC, c0:c0 + LANES]
            y_ref[r0:r0 + CONV_RC, c0:c0 + LANES] = acc
    y = y_ref[...] + db_ref[...]
    mu = jnp.mean(y, axis=-1, keepdims=True)
    yc = y - mu
    var = jnp.mean(yc * yc, axis=-1, keepdims=True)
    z = yc * lax.rsqrt(var + EPS) * g_ref[...] + b_ref[...]
    z = z * jax.nn.sigmoid(z)
    o_ref[...] = _dot(z.astype(BF16), pw_ref[...]).astype(o_ref.dtype)


def _conv(u, dw, db, ln_g, ln_b, pw):
    B, L, C = u.shape
    cw = dw.shape[0]
    assert cw // 2 <= CONV_HALO
    tt = _pick(L, 256, CONV_RC)
    nt = L // tt
    hb = tt // CONV_HALO
    nh = L // CONV_HALO
    cur = pl.BlockSpec((None, tt, C), lambda b, i: (b, i, 0))
    prev = pl.BlockSpec((None, CONV_HALO, C), lambda b, i: (b, jnp.maximum(i * hb - 1, 0), 0))
    nxt = pl.BlockSpec((None, CONV_HALO, C), lambda b, i: (b, jnp.minimum((i + 1) * hb, nh - 1), 0))
    vec = pl.BlockSpec((1, C), lambda b, i: (0, 0))
    return pl.pallas_call(
        functools.partial(_conv_kernel, cw, nt), grid=(B, nt),
        in_specs=[prev, cur, nxt, pl.BlockSpec((cw, C), lambda b, i: (0, 0)), vec, vec, vec,
                  pl.BlockSpec((C, C), lambda b, i: (0, 0))],
        out_specs=cur, out_shape=jax.ShapeDtypeStruct((B, L, C), BF16),
        scratch_shapes=[pltpu.VMEM((tt + 2 * CONV_HALO, C), F32), pltpu.VMEM((tt, C), F32)],
        compiler_params=_cparams(("parallel", "parallel")), name="conv",
    )(u, u, u, dw.astype(F32), db.reshape(1, C).astype(F32), ln_g.reshape(1, C).astype(F32),
      ln_b.reshape(1, C).astype(F32), pw)


def _outproj_kernel(ka, kb, a_ref, b_ref, c_ref, w_ref, x_ref, g_ref, o_ref):
    acc = _dot(a_ref[...], w_ref[0:ka, :])
    acc = acc + _dot(b_ref[...], w_ref[ka:ka + kb, :])
    acc = acc + _dot(c_ref[...], w_ref[ka + kb:, :])
    o_ref[...] = x_ref[...] + g_ref[...] * acc


def _outproj(a, b, c, w, x, gate):
    B, L, D = x.shape
    ka, kb, kc = a.shape[2], b.shape[2], c.shape[2]
    K = ka + kb + kc
    tm = _pick(L, 1024, 8)
    tn = _pick(D, 512)
    grid = (B, L // tm, D // tn)

    def aspec(k):
        return pl.BlockSpec((None, tm, k), lambda bb, i, j: (bb, i, 0))

    xspec = pl.BlockSpec((None, tm, tn), lambda bb, i, j: (bb, i, j))
    return pl.pallas_call(
        functools.partial(_outproj_kernel, ka, kb), grid=grid,
        in_specs=[aspec(ka), aspec(kb), aspec(kc),
                  pl.BlockSpec((K, tn), lambda bb, i, j: (0, j)), xspec,
                  pl.BlockSpec((None, 1, tn), lambda bb, i, j: (bb, 0, j))],
        out_specs=xspec, out_shape=jax.ShapeDtypeStruct((B, L, D), F32),
        compiler_params=_cparams(("parallel", "parallel", "arbitrary")), name="outproj",
    )(a, b, c, w, x, gate)


def _ffn_kernel(xs_ref, wg_ref, wu_ref, wd_ref, g_ref, o_ref):
    xs = xs_ref[...]
    a = _dot(xs, wg_ref[...])
    u = _dot(xs, wu_ref[...])
    hmid = (a * jax.nn.sigmoid(a) * u).astype(BF16)
    o_ref[...] = _dot(hmid, wd_ref[...]) * g_ref[...]


def _experts(xs, wg, wu, wd, g):
    B, E, C, D = xs.shape
    Fd = wg.shape[2]
    tc = _pick(C, 256, 8)
    grid = (E, B, C // tc)
    return pl.pallas_call(
        _ffn_kernel, grid=grid,
        in_specs=[pl.BlockSpec((None, None, tc, D), lambda e, b, i: (b, e, i, 0)),
                  pl.BlockSpec((None, D, Fd), lambda e, b, i: (e, 0, 0)),
                  pl.BlockSpec((None, D, Fd), lambda e, b, i: (e, 0, 0)),
                  pl.BlockSpec((None, Fd, D), lambda e, b, i: (e, 0, 0)),
                  pl.BlockSpec((None, None, tc, 1), lambda e, b, i: (b, e, i, 0))],
        out_specs=pl.BlockSpec((None, None, tc, D), lambda e, b, i: (b, e, i, 0)),
        out_shape=jax.ShapeDtypeStruct((B, E, C, D), F32),
        compiler_params=_cparams(("parallel", "parallel", "arbitrary")), name="experts",
    )(xs, wg, wu, wd, g)


def _ec_moe(xres, gate, scale, shift, w_router, wg, wu, wd):
    B, L, D = xres.shape
    E = w_router.shape[1]
    cap = max(1, EC_CAPACITY * L // E)
    h2, logits = _norm_mod(xres, scale, shift, w_router)
    aff = jax.nn.softmax(logits[..., :E], axis=-1)
    g, idx = lax.top_k(jnp.swapaxes(aff, 1, 2), cap)
    bidx = jnp.arange(B)[:, None, None]
    xs = h2[bidx, idx]
    y = _experts(xs, wg, wu, wd, g[..., None])
    moe = jnp.zeros((B, L, D), F32).at[bidx, idx].add(y)
    return xres + gate * moe


def _ada(cvec, down, up, bias, n, D):
    hi = lax.Precision.HIGHEST
    z = jnp.dot(jax.nn.silu(cvec), down, precision=hi)
    m = jnp.dot(z, up[:, :n * D], precision=hi) + bias[:n * D]
    return m.reshape(cvec.shape[:-1] + (n, D))


def _rope_tables(T):
    nf = DIFF_DIM // 4
    inv = ROPE_BASE ** (-jnp.arange(nf, dtype=F32) / nf)
    t_idx = jnp.arange(T)
    ang_r = (t_idx // GRID_W).astype(F32)[:, None] * inv
    ang_c = (t_idx % GRID_W).astype(F32)[:, None] * inv
    cr, sr, cc, sc = jnp.cos(ang_r), jnp.sin(ang_r), jnp.cos(ang_c), jnp.sin(ang_c)
    cos = jnp.concatenate([cr, cr, cc, cc] * 2, axis=-1)
    sin = jnp.concatenate([-sr, sr, -sc, sc] * 2, axis=-1)
    return cos, sin


def kernel(x, c, ctx, c_ctx, ada_down, ada_up, ada_bias, w_in, conv_dw, conv_db, conv_ln_g,
           conv_ln_b, conv_pw, na_q_gain, na_k_gain, na_rpb, diff_q_gain, diff_k_gain, diff_lam,
           diff_out_gain, w_out, w_router, w_gate, w_up, w_down):
    B, T, D = x.shape
    Lc = ctx.shape[1]
    depth = w_in.shape[0]
    na_w = 3 * D // 8
    df_w = 3 * D // 8
    cch = D // 4
    o_kna, o_vna, o_kdf, o_vdf = 0, na_w, 2 * na_w, 2 * na_w + df_w
    o_qna = 2 * na_w + 2 * df_w
    o_qdf = o_qna + na_w
    o_cv = o_qdf + df_w
    o_cg = o_cv + cch
    rows = T // GRID_W
    cos, sin = _rope_tables(T)

    for l in range(depth):
        update_ctx = l < depth - 1
        lam_init = 0.8 - 0.6 * math.exp(-0.3 * l)
        lv = diff_lam[l].astype(F32)
        lam = jnp.exp(jnp.sum(lv[0] * lv[1])) - jnp.exp(jnp.sum(lv[2] * lv[3])) + lam_init
        post = 1.0 - lam_init
        m = _ada(c, ada_down[l], ada_up[l], ada_bias[l], N_MOD, D)
        mc = _ada(c_ctx, ada_down[l], ada_up[l], ada_bias[l], N_MOD if update_ctx else 2, D)
        mcb = jnp.broadcast_to(mc[None], (B,) + mc.shape)
        wl = w_in[l].astype(BF16)
        wo = w_out[l].astype(BF16)
        pw = conv_pw[l].astype(BF16)
        wg, wu, wd = w_gate[l].astype(BF16), w_up[l].astype(BF16), w_down[l].astype(BF16)
        qg_df = diff_q_gain[l].reshape(-1)
        kg_df = diff_k_gain[l].reshape(-1)

        hc = _norm_mod(ctx, mcb[:, 1:2], mcb[:, 0:1]).reshape(B * Lc, D)
        kna_c = _proj(hc, wl, o_kna, na_w, "rms128", gain=na_k_gain[l]).reshape(B, Lc, na_w)
        vna_c = _proj(hc, wl, o_vna, na_w, "plain").reshape(B, Lc, na_w)
        kdf_c = _proj(hc, wl, o_kdf, df_w, "rms64", gain=kg_df).reshape(B, Lc, df_w)
        vdf_c = _proj(hc, wl, o_vdf, df_w, "plain").reshape(B, Lc, df_w)

        h = _norm_mod(x, m[:, 1:2], m[:, 0:1]).reshape(B * T, D)
        k_na = _proj(h, wl, o_kna, na_w, "rms128", gain=na_k_gain[l]).reshape(B, T, na_w)
        v_na = _proj(h, wl, o_vna, na_w, "plain").reshape(B, T, na_w)
        k_df = _proj(h, wl, o_kdf, df_w, "rms64", gain=kg_df, cos=cos, sin=sin).reshape(B, T, df_w)
        v_df = _proj(h, wl, o_vdf, df_w, "plain").reshape(B, T, df_w)
        q_na = _proj(h, wl, o_qna, na_w, "rms128", gain=na_q_gain[l]).reshape(B, T, na_w)
        q_df = _proj(h, wl, o_qdf, df_w, "rms64", gain=qg_df, cos=cos, sin=sin).reshape(B, T, df_w)
        u = _proj(h, wl, o_cv, cch, "glu", col0_b=o_cg).reshape(B, T, cch)

        o_na = _na_attn(q_na, k_na, v_na, kna_c, vna_c, _na_bias(na_rpb[l], rows))
        o_df = _diff_attn(q_df, kdf_c, vdf_c, k_df, v_df, lam, diff_out_gain[l], post)
        o_cv_ = _conv(u, conv_dw[l], conv_db[l], conv_ln_g[l], conv_ln_b[l], pw)
        x = _outproj(o_cv_, o_na, o_df, wo, x, m[:, 2:3])
        x = _ec_moe(x, m[:, 5:6], m[:, 4:5], m[:, 3:4], w_router[l], wg, wu, wd)

        if update_ctx:
            qna_c = _proj(hc, wl, o_qna, na_w, "rms128", gain=na_q_gain[l]).reshape(B, Lc, na_w)
            qdf_c = _proj(hc, wl, o_qdf, df_w, "rms64", gain=qg_df).reshape(B, Lc, df_w)
            u_c = _proj(hc, wl, o_cv, cch, "glu", col0_b=o_cg).reshape(B, Lc, cch)
            o_na_c = _dense_attn(qna_c, kna_c, vna_c)
            o_df_c = _diff_attn(qdf_c, kdf_c, vdf_c, None, None, lam, diff_out_gain[l], post)
            o_cv_c = _conv(u_c, conv_dw[l], conv_db[l], conv_ln_g[l], conv_ln_b[l], pw)
            ctx = _outproj(o_cv_c, o_na_c, o_df_c, wo, ctx, mcb[:, 2:3])
            ctx = _ec_moe(ctx, mcb[:, 5:6], mcb[:, 4:5], mcb[:, 3:4], w_router[l], wg, wu, wd)
    return x
```

```python
import functools
import math

import numpy as np
import jax
import jax.numpy as jnp
from jax import lax
from jax.experimental import pallas as pl
from jax.experimental.pallas import tpu as pltpu

GRID_W = 64
HEAD_DIM = 128
WIN_R = 8
WIN_C = 16
DIFF_DIM = HEAD_DIM // 2
ROPE_BASE = 10000.0
EC_CAPACITY = 2
N_MOD = 6
EPS = 1e-6
LANES = 128
NA_QROWS = 4
NEG = -1e30
VMEM_LIMIT = 56 * 1024 * 1024

F32 = jnp.float32
BF16 = jnp.bfloat16


def _cparams(sem):
    return pltpu.CompilerParams(dimension_semantics=sem, vmem_limit_bytes=VMEM_LIMIT)


def _pick(n, cap, mult=LANES):
    best = None
    for t in range(mult, min(n, cap) + 1, mult):
        if n % t == 0:
            best = t
    assert best is not None, (n, cap, mult)
    return best


def _dot(a, b):
    return jnp.dot(a, b, preferred_element_type=F32)


def _dot_nt(a, b):
    return lax.dot_general(a, b, (((1,), (1,)), ((), ())), preferred_element_type=F32)


def _norm_kernel(x_ref, sc_ref, sh_ref, o_ref):
    x = x_ref[...]
    ms = jnp.mean(x * x, axis=-1, keepdims=True)
    h = x * lax.rsqrt(ms + EPS) * (1.0 + sc_ref[...]) + sh_ref[...]
    o_ref[...] = h.astype(o_ref.dtype)


def _norm_router_kernel(x_ref, sc_ref, sh_ref, wr_ref, o_ref, lg_ref):
    x = x_ref[...]
    ms = jnp.mean(x * x, axis=-1, keepdims=True)
    h = (x * lax.rsqrt(ms + EPS) * (1.0 + sc_ref[...]) + sh_ref[...]).astype(o_ref.dtype)
    o_ref[...] = h
    lg_ref[...] = _dot(h, wr_ref[...])


def _norm_mod(x, scale, shift, w_router=None):
    B, L, D = x.shape
    tl = _pick(L, 512, 8)
    grid = (B, L // tl)
    xspec = pl.BlockSpec((None, tl, D), lambda b, i: (b, i, 0))
    mspec = pl.BlockSpec((None, 1, D), lambda b, i: (b, 0, 0))
    if w_router is None:
        return pl.pallas_call(
            _norm_kernel, grid=grid, in_specs=[xspec, mspec, mspec], out_specs=xspec,
            out_shape=jax.ShapeDtypeStruct((B, L, D), BF16),
            compiler_params=_cparams(("parallel", "parallel")), name="norm_mod",
        )(x, scale, shift)
    E = w_router.shape[1]
    wr = jnp.zeros((D, LANES), BF16).at[:, :E].set(w_router.astype(BF16))
    return pl.pallas_call(
        _norm_router_kernel, grid=grid,
        in_specs=[xspec, mspec, mspec, pl.BlockSpec((D, LANES), lambda b, i: (0, 0))],
        out_specs=[xspec, pl.BlockSpec((None, tl, LANES), lambda b, i: (b, i, 0))],
        out_shape=[jax.ShapeDtypeStruct((B, L, D), BF16),
                   jax.ShapeDtypeStruct((B, L, LANES), F32)],
        compiler_params=_cparams(("parallel", "parallel")), name="norm_router",
    )(x, scale, shift, wr)


def _proj_kernel(kind, rope, tn, *refs):
    if kind == "glu":
        h_ref, w_ref, w2_ref, o_ref = refs
        h = h_ref[...]
        val = _dot(h, w_ref[...])
        gate = _dot(h, w2_ref[...])
        o_ref[...] = val * jax.nn.sigmoid(gate)
        return
    if kind == "plain":
        h_ref, w_ref, o_ref = refs
        o_ref[...] = _dot(h_ref[...], w_ref[...]).astype(o_ref.dtype)
        return
    if rope:
        h_ref, w_ref, g_ref, cos_ref, sin_ref, o_ref = refs
    else:
        h_ref, w_ref, g_ref, o_ref = refs
    acc = _dot(h_ref[...], w_ref[...])
    for s in range(tn // LANES):
        t = acc[:, s * LANES:(s + 1) * LANES]
        sq = t * t
        lane = lax.broadcasted_iota(jnp.int32, t.shape, 1)
        if kind == "rms128":
            inv = lax.rsqrt(jnp.mean(sq, axis=-1, keepdims=True) + EPS)
        else:
            lo = jnp.sum(jnp.where(lane < DIFF_DIM, sq, 0.0), axis=-1, keepdims=True)
            tot = jnp.sum(sq, axis=-1, keepdims=True)
            inv = jnp.where(lane < DIFF_DIM,
                            lax.rsqrt(lo * (1.0 / DIFF_DIM) + EPS),
                            lax.rsqrt((tot - lo) * (1.0 / DIFF_DIM) + EPS))
        t = t * inv * g_ref[...]
        if rope:
            up = pltpu.roll(t, LANES - 16, axis=1)
            dn = pltpu.roll(t, 16, axis=1)
            sw = jnp.where((lane % 32) < 16, up, dn)
            t = t * cos_ref[...] + sw * sin_ref[...]
        o_ref[:, s * LANES:(s + 1) * LANES] = t.astype(o_ref.dtype)


def _proj(h, w, col0, ncols, kind, *, gain=None, cos=None, sin=None, col0_b=None):
    M, D = h.shape
    tm = _pick(M, 1024, 8)
    tn = _pick(ncols, 512)
    assert col0 % tn == 0
    grid = (M // tm, ncols // tn)
    hspec = pl.BlockSpec((tm, D), lambda i, j: (i, 0))
    c0 = col0 // tn
    wspec = pl.BlockSpec((D, tn), lambda i, j: (0, c0 + j))
    ospec = pl.BlockSpec((tm, tn), lambda i, j: (i, j))
    rope = cos is not None
    ins, specs = [h, w], [hspec, wspec]
    odt = BF16
    if kind == "glu":
        assert col0_b % tn == 0
        c1 = col0_b // tn
        ins.append(w)
        specs.append(pl.BlockSpec((D, tn), lambda i, j: (0, c1 + j)))
        odt = F32
    elif kind != "plain":
        ins.append(gain.reshape(1, LANES).astype(F32))
        specs.append(pl.BlockSpec((1, LANES), lambda i, j: (0, 0)))
        if rope:
            nt = cos.shape[0] // tm
            tspec = pl.BlockSpec((tm, LANES), lambda i, j: (i % nt, 0))
            ins += [cos, sin]
            specs += [tspec, tspec]
    return pl.pallas_call(
        functools.partial(_proj_kernel, kind, rope, tn), grid=grid,
        in_specs=specs, out_specs=ospec,
        out_shape=jax.ShapeDtypeStruct((M, ncols), odt),
        compiler_params=_cparams(("parallel", "arbitrary")), name="proj_" + kind,
    )(*ins)


def _na_bias(rpb):
    L, H = rpb.shape[:2]
    qc = np.arange(GRID_W)
    kc = np.arange(GRID_W)
    cs = np.clip(qc - WIN_C // 2, 0, GRID_W - WIN_C)
    col_ok = (kc[None, :] >= cs[:, None]) & (kc[None, :] < cs[:, None] + WIN_C)
    dc = kc[None, :] - qc[:, None] + WIN_C - 1
    onehot = (dc[None] == np.arange(2 * WIN_C - 1)[:, None, None]) & col_ok[None]
    tq = jnp.einsum("lhrc,cqk->lhqrk", rpb.astype(F32), jnp.asarray(onehot, F32),
                    precision=lax.Precision.HIGHEST)
    tq = jnp.where(jnp.asarray(col_ok)[:, None, :], tq, NEG)
    nkr = 3 * NA_QROWS

    def neg(n):
        return jnp.full((L, H, GRID_W, n, GRID_W), NEG, F32)

    tabs = []
    for typ in range(3):
        rows_q = []
        for qr in range(NA_QROWS):
            if typ == 0:
                k0, d0 = 0, WIN_R - 1 - qr
            elif typ == 1:
                k0, d0 = qr, WIN_R // 2 - 1
            else:
                k0, d0 = nkr - WIN_R, nkr - 2 * NA_QROWS - 1 - qr
            parts = [tq[:, :, :, d0:d0 + WIN_R, :]]
            if k0:
                parts.insert(0, neg(k0))
            if nkr - k0 - WIN_R:
                parts.append(neg(nkr - k0 - WIN_R))
            rows_q.append(jnp.concatenate(parts, axis=3).reshape(L, H, GRID_W, nkr * GRID_W))
        tabs.append(jnp.concatenate(rows_q, axis=2))
    return jnp.stack(tabs, axis=1)


def _na_kernel(scale, q_ref, k0_ref, k1_ref, k2_ref, v0_ref, v1_ref, v2_ref,
               kc_ref, vc_ref, b_ref, o_ref):
    q = q_ref[...]
    nq = q.shape[0]
    krefs = (k0_ref, k1_ref, k2_ref)
    vrefs = (v0_ref, v1_ref, v2_ref)
    s_loc = [_dot_nt(q, krefs[d][...]) * scale + b_ref[:, d * nq:(d + 1) * nq] for d in range(3)]
    s_ctx = _dot_nt(q, kc_ref[...]) * scale
    m = jnp.max(s_ctx, axis=-1, keepdims=True)
    for s in s_loc:
        m = jnp.maximum(m, jnp.max(s, axis=-1, keepdims=True))
    p_ctx = jnp.exp(s_ctx - m)
    l = jnp.sum(p_ctx, axis=-1, keepdims=True)
    o = _dot(p_ctx.astype(BF16), vc_ref[...])
    for d in range(3):
        p = jnp.exp(s_loc[d] - m)
        l = l + jnp.sum(p, axis=-1, keepdims=True)
        o = o + _dot(p.astype(BF16), vrefs[d][...])
    o_ref[...] = (o / l).astype(o_ref.dtype)


def _na_attn(q, k, v, kc, vc, bias):
    B, T, W = q.shape
    Lc = kc.shape[1]
    H = W // HEAD_DIM
    nq = NA_QROWS * GRID_W
    nblk = T // nq
    assert T % nq == 0 and nblk >= 4
    grid = (H, B, nblk)

    def kmap(d):
        return lambda h, b, i: (b, jnp.clip(i - 1, 0, nblk - 3) + d, h)

    qspec = pl.BlockSpec((None, nq, HEAD_DIM), lambda h, b, i: (b, i, h))
    kspecs = [pl.BlockSpec((None, nq, HEAD_DIM), kmap(d)) for d in range(3)]
    cspec = pl.BlockSpec((None, Lc, HEAD_DIM), lambda h, b, i: (b, 0, h))
    bspec = pl.BlockSpec(
        (None, None, nq, 3 * nq),
        lambda h, b, i: (jnp.where(i == 0, 0, jnp.where(i == nblk - 1, 2, 1)), h, 0, 0))
    return pl.pallas_call(
        functools.partial(_na_kernel, HEAD_DIM ** -0.5), grid=grid,
        in_specs=[qspec] + kspecs + kspecs + [cspec, cspec, bspec], out_specs=qspec,
        out_shape=jax.ShapeDtypeStruct((B, T, W), BF16),
        compiler_params=_cparams(("parallel", "parallel", "arbitrary")), name="na_attn",
    )(q, k, k, k, v, v, v, kc, vc, bias)


def _dense_kernel(scale, q_ref, k_ref, v_ref, o_ref):
    s = _dot_nt(q_ref[...], k_ref[...]) * scale
    m = jnp.max(s, axis=-1, keepdims=True)
    p = jnp.exp(s - m)
    l = jnp.sum(p, axis=-1, keepdims=True)
    o_ref[...] = (_dot(p.astype(BF16), v_ref[...]) / l).astype(o_ref.dtype)


def _dense_attn(q, k, v):
    B, L, W = q.shape
    H = W // HEAD_DIM
    spec = pl.BlockSpec((None, L, HEAD_DIM), lambda b, h: (b, 0, h))
    return pl.pallas_call(
        functools.partial(_dense_kernel, HEAD_DIM ** -0.5), grid=(B, H),
        in_specs=[spec, spec, spec], out_specs=spec,
        out_shape=jax.ShapeDtypeStruct((B, L, W), BF16),
        compiler_params=_cparams(("parallel", "parallel")), name="dense_attn",
    )(q, k, v)


def _diff_kernel(tk, n_lat, post_scale, lam_ref, q_ref, kc_ref, vc_ref, *refs):
    if n_lat:
        k_ref, v_ref, g_ref, o_ref, qs_sc, m_sc, acc_sc, vcx_sc, vx_sc, sa_sc, sb_sc = refs
    else:
        g_ref, o_ref, qs_sc, m_sc, acc_sc, vcx_sc = refs
    q = q_ref[...]
    tq = q.shape[0]

    def with_ones(dst, src):
        lane = lax.broadcasted_iota(jnp.int32, src.shape, 1)
        dst[:, 0:HEAD_DIM] = src[...]
        dst[:, HEAD_DIM:] = jnp.where(lane == 0, 1.0, 0.0).astype(BF16)

    @pl.when(pl.program_id(2) == 0)
    def _():
        with_ones(vcx_sc, vc_ref)
        if n_lat:
            with_ones(vx_sc, v_ref)

    lane = lax.broadcasted_iota(jnp.int32, q.shape, 1)
    zero = jnp.zeros_like(q)
    qs_sc[0:tq, :] = jnp.where(lane < DIFF_DIM, q, zero)
    qs_sc[tq:, :] = jnp.where(lane >= DIFF_DIM, q, zero)
    m_sc[...] = jnp.full(m_sc.shape, -jnp.inf, F32)
    acc_sc[...] = jnp.zeros(acc_sc.shape, F32)

    def softmax_pv(s, vb):
        m_old = m_sc[...]
        m_new = jnp.maximum(m_old, jnp.max(s, axis=-1, keepdims=True))
        alpha = jnp.exp2(m_old - m_new)
        p = jnp.exp2((s - pltpu.repeat(m_new, s.shape[1] // LANES, axis=1)).astype(BF16))
        acc_sc[...] = pltpu.repeat(alpha, 2, axis=1) * acc_sc[...] + _dot(p, vb)
        m_sc[...] = m_new

    softmax_pv(_dot_nt(qs_sc[...], kc_ref[...]), vcx_sc[...])
    if n_lat:
        sa_sc[...] = _dot_nt(qs_sc[...], k_ref[pl.ds(0, tk), :])

        def body(c2, carry):
            c = 2 * c2
            off0 = pl.multiple_of(c * tk, tk)
            off1 = pl.multiple_of((c + 1) * tk, tk)
            off2 = pl.multiple_of(jnp.minimum(c + 2, n_lat - 1) * tk, tk)
            sb_sc[...] = _dot_nt(qs_sc[...], k_ref[pl.ds(off1, tk), :])
            softmax_pv(sa_sc[...], vx_sc[pl.ds(off0, tk), :])
            sa_sc[...] = _dot_nt(qs_sc[...], k_ref[pl.ds(off2, tk), :])
            softmax_pv(sb_sc[...], vx_sc[pl.ds(off1, tk), :])
            return carry

        lax.fori_loop(0, n_lat // 2, body, 0)
    acc = acc_sc[...]
    o = acc[:, :HEAD_DIM] / acc[:, HEAD_DIM:HEAD_DIM + 1]
    o = o[:tq] - lam_ref[0, 0] * o[tq:]
    o = o * lax.rsqrt(jnp.mean(o * o, axis=-1, keepdims=True) + EPS)
    o_ref[...] = (o * g_ref[...] * post_scale).astype(o_ref.dtype)


def _diff_attn(q, kc, vc, k, v, lam, gain, post_scale):
    B, L, W = q.shape
    Lc = kc.shape[1]
    H = W // HEAD_DIM
    tq = _pick(L, 512, 8)
    grid = (B, H, L // tq)
    qspec = pl.BlockSpec((None, tq, HEAD_DIM), lambda b, h, i: (b, i, h))
    cspec = pl.BlockSpec((None, Lc, HEAD_DIM), lambda b, h, i: (b, 0, h))
    ins = [lam.reshape(1, 1).astype(F32), q, kc, vc]
    specs = [pl.BlockSpec(memory_space=pltpu.SMEM), qspec, cspec, cspec]
    scratch = [pltpu.VMEM((2 * tq, HEAD_DIM), BF16), pltpu.VMEM((2 * tq, LANES), F32),
               pltpu.VMEM((2 * tq, 2 * HEAD_DIM), F32), pltpu.VMEM((Lc, 2 * HEAD_DIM), BF16)]
    tk, n_lat = 0, 0
    if k is not None:
        T = k.shape[1]
        tk = _pick(T, 1024, 8)
        n_lat = T // tk
        assert n_lat % 2 == 0
        lspec = pl.BlockSpec((None, T, HEAD_DIM), lambda b, h, i: (b, 0, h))
        ins += [k, v]
        specs += [lspec, lspec]
        scratch += [pltpu.VMEM((T, 2 * HEAD_DIM), BF16),
                    pltpu.VMEM((2 * tq, tk), F32), pltpu.VMEM((2 * tq, tk), F32)]
    ins.append(gain.reshape(1, HEAD_DIM).astype(F32))
    specs.append(pl.BlockSpec((1, HEAD_DIM), lambda b, h, i: (0, 0)))
    return pl.pallas_call(
        functools.partial(_diff_kernel, tk, n_lat, post_scale), grid=grid,
        in_specs=specs, out_specs=qspec,
        out_shape=jax.ShapeDtypeStruct((B, L, W), BF16), scratch_shapes=scratch,
        compiler_params=_cparams(("parallel", "parallel", "arbitrary")), name="diff_attn",
    )(*ins)


CONV_HALO = 16
CONV_RC = 32


def _conv_kernel(cw, nt, prev_ref, cur_ref, next_ref, dw_ref, db_ref, g_ref, b_ref,
                 pw_ref, o_ref, pad_ref, y_ref):
    i = pl.program_id(1)
    tt, ch = cur_ref.shape
    half = cw // 2
    pad_ref[0:CONV_HALO, :] = jnp.where(i == 0, 0.0, prev_ref[...])
    pad_ref[CONV_HALO:CONV_HALO + tt, :] = cur_ref[...]
    pad_ref[CONV_HALO + tt:, :] = jnp.where(i == nt - 1, 0.0, next_ref[...])
    base = CONV_HALO - half
    for r0 in range(0, tt, CONV_RC):
        for c0 in range(0, ch, LANES):
            acc = jnp.zeros((CONV_RC, LANES), F32)
            for j in range(cw):
                acc = acc + dw_ref[j:j + 1, c0:c0 + LANES] * \
                    pad_ref[r0 + base + j:r0 + base + j + CONV_RC, c0:c0 + LANES]
            y_ref[r0:r0 + CONV_RC, c0:c0 + LANES] = acc
    y = y_ref[...] + db_ref[...]
    mu = jnp.mean(y, axis=-1, keepdims=True)
    yc = y - mu
    var = jnp.mean(yc * yc, axis=-1, keepdims=True)
    z = yc * lax.rsqrt(var + EPS) * g_ref[...] + b_ref[...]
    z = z * jax.nn.sigmoid(z)
    o_ref[...] = _dot(z.astype(BF16), pw_ref[...]).astype(o_ref.dtype)


def _conv(u, dw, db, ln_g, ln_b, pw):
    B, L, C = u.shape
    cw = dw.shape[0]
    assert cw // 2 <= CONV_HALO
    tt = _pick(L, 256, CONV_RC)
    nt = L // tt
    hb = tt // CONV_HALO
    nh = L // CONV_HALO
    cur = pl.BlockSpec((None, tt, C), lambda b, i: (b, i, 0))
    prev = pl.BlockSpec((None, CONV_HALO, C), lambda b, i: (b, jnp.maximum(i * hb - 1, 0), 0))
    nxt = pl.BlockSpec((None, CONV_HALO, C), lambda b, i: (b, jnp.minimum((i + 1) * hb, nh - 1), 0))
    vec = pl.BlockSpec((1, C), lambda b, i: (0, 0))
    return pl.pallas_call(
        functools.partial(_conv_kernel, cw, nt), grid=(B, nt),
        in_specs=[prev, cur, nxt, pl.BlockSpec((cw, C), lambda b, i: (0, 0)), vec, vec, vec,
                  pl.BlockSpec((C, C), lambda b, i: (0, 0))],
        out_specs=cur, out_shape=jax.ShapeDtypeStruct((B, L, C), BF16),
        scratch_shapes=[pltpu.VMEM((tt + 2 * CONV_HALO, C), F32), pltpu.VMEM((tt, C), F32)],
        compiler_params=_cparams(("parallel", "parallel")), name="conv",
    )(u, u, u, dw.astype(F32), db.reshape(1, C).astype(F32), ln_g.reshape(1, C).astype(F32),
      ln_b.reshape(1, C).astype(F32), pw)


def _outproj_kernel(ka, kb, a_ref, b_ref, c_ref, w_ref, x_ref, g_ref, o_ref):
    acc = _dot(a_ref[...], w_ref[0:ka, :])
    acc = acc + _dot(b_ref[...], w_ref[ka:ka + kb, :])
    acc = acc + _dot(c_ref[...], w_ref[ka + kb:, :])
    o_ref[...] = x_ref[...] + g_ref[...] * acc


def _outproj(a, b, c, w, x, gate):
    B, L, D = x.shape
    ka, kb, kc = a.shape[2], b.shape[2], c.shape[2]
    K = ka + kb + kc
    tm = _pick(L, 1024, 8)
    tn = _pick(D, 512)
    grid = (B, L // tm, D // tn)

    def aspec(k):
        return pl.BlockSpec((None, tm, k), lambda bb, i, j: (bb, i, 0))

    xspec = pl.BlockSpec((None, tm, tn), lambda bb, i, j: (bb, i, j))
    return pl.pallas_call(
        functools.partial(_outproj_kernel, ka, kb), grid=grid,
        in_specs=[aspec(ka), aspec(kb), aspec(kc),
                  pl.BlockSpec((K, tn), lambda bb, i, j: (0, j)), xspec,
                  pl.BlockSpec((None, 1, tn), lambda bb, i, j: (bb, 0, j))],
        out_specs=xspec, out_shape=jax.ShapeDtypeStruct((B, L, D), F32),
        compiler_params=_cparams(("parallel", "parallel", "arbitrary")), name="outproj",
    )(a, b, c, w, x, gate)


def _ffn_kernel(xs_ref, wg_ref, wu_ref, wd_ref, g_ref, o_ref):
    xs = xs_ref[...]
    a = _dot(xs, wg_ref[...])
    u = _dot(xs, wu_ref[...])
    hmid = (a * jax.nn.sigmoid(a) * u).astype(BF16)
    o_ref[...] = _dot(hmid, wd_ref[...]) * g_ref[...]


def _experts(xs, wg, wu, wd, g):
    B, E, C, D = xs.shape
    Fd = wg.shape[2]
    tc = _pick(C, 256, 8)
    grid = (E, B, C // tc)
    return pl.pallas_call(
        _ffn_kernel, grid=grid,
        in_specs=[pl.BlockSpec((None, None, tc, D), lambda e, b, i: (b, e, i, 0)),
                  pl.BlockSpec((None, D, Fd), lambda e, b, i: (e, 0, 0)),
                  pl.BlockSpec((None, D, Fd), lambda e, b, i: (e, 0, 0)),
                  pl.BlockSpec((None, Fd, D), lambda e, b, i: (e, 0, 0)),
                  pl.BlockSpec((None, None, tc, 1), lambda e, b, i: (b, e, i, 0))],
        out_specs=pl.BlockSpec((None, None, tc, D), lambda e, b, i: (b, e, i, 0)),
        out_shape=jax.ShapeDtypeStruct((B, E, C, D), F32),
        compiler_params=_cparams(("parallel", "parallel", "arbitrary")), name="experts",
    )(xs, wg, wu, wd, g)


def _ec_moe(xres, gate, scale, shift, w_router, wg, wu, wd):
    B, L, D = xres.shape
    E = w_router.shape[1]
    cap = max(1, EC_CAPACITY * L // E)
    h2, logits = _norm_mod(xres, scale, shift, w_router)
    aff = jax.nn.softmax(logits[..., :E], axis=-1)
    g, idx = lax.top_k(jnp.swapaxes(aff, 1, 2), cap)
    bidx = jnp.arange(B)[:, None, None]
    xs = h2[bidx, idx]
    y = _experts(xs, wg, wu, wd, g[..., None])
    moe = jnp.zeros((B, L, D), F32).at[bidx, idx].add(y)
    return xres + gate * moe


def _ada(cvec, down, up, bias, n, D):
    hi = lax.Precision.HIGHEST
    z = jnp.dot(jax.nn.silu(cvec), down, precision=hi)
    m = jnp.dot(z, up[:, :n * D], precision=hi) + bias[:n * D]
    return m.reshape(cvec.shape[:-1] + (n, D))


def _rope_tables(T):
    nf = DIFF_DIM // 4
    inv = ROPE_BASE ** (-jnp.arange(nf, dtype=F32) / nf)
    t_idx = jnp.arange(T)
    ang_r = (t_idx // GRID_W).astype(F32)[:, None] * inv
    ang_c = (t_idx % GRID_W).astype(F32)[:, None] * inv
    cr, sr, cc, sc = jnp.cos(ang_r), jnp.sin(ang_r), jnp.cos(ang_c), jnp.sin(ang_c)
    cos = jnp.concatenate([cr, cr, cc, cc] * 2, axis=-1)
    sin = jnp.concatenate([-sr, sr, -sc, sc] * 2, axis=-1)
    return cos, sin


def kernel(x, c, ctx, c_ctx, ada_down, ada_up, ada_bias, w_in, conv_dw, conv_db, conv_ln_g,
           conv_ln_b, conv_pw, na_q_gain, na_k_gain, na_rpb, diff_q_gain, diff_k_gain, diff_lam,
           diff_out_gain, w_out, w_router, w_gate, w_up, w_down):
    B, T, D = x.shape
    Lc = ctx.shape[1]
    depth = w_in.shape[0]
    na_w = 3 * D // 8
    df_w = 3 * D // 8
    cch = D // 4
    o_kna, o_vna, o_kdf, o_vdf = 0, na_w, 2 * na_w, 2 * na_w + df_w
    o_qna = 2 * na_w + 2 * df_w
    o_qdf = o_qna + na_w
    o_cv = o_qdf + df_w
    o_cg = o_cv + cch
    cos, sin = _rope_tables(T)
    na_bias = _na_bias(na_rpb)

    for l in range(depth):
        update_ctx = l < depth - 1
        lam_init = 0.8 - 0.6 * math.exp(-0.3 * l)
        lv = diff_lam[l].astype(F32)
        lam = jnp.exp(jnp.sum(lv[0] * lv[1])) - jnp.exp(jnp.sum(lv[2] * lv[3])) + lam_init
        post = 1.0 - lam_init
        m = _ada(c, ada_down[l], ada_up[l], ada_bias[l], N_MOD, D)
        mc = _ada(c_ctx, ada_down[l], ada_up[l], ada_bias[l], N_MOD if update_ctx else 2, D)
        mcb = jnp.broadcast_to(mc[None], (B,) + mc.shape)
        wl = w_in[l].astype(BF16)
        wo = w_out[l].astype(BF16)
        pw = conv_pw[l].astype(BF16)
        wg, wu, wd = w_gate[l].astype(BF16), w_up[l].astype(BF16), w_down[l].astype(BF16)
        qg_df = diff_q_gain[l].reshape(-1) * (DIFF_DIM ** -0.5 * math.log2(math.e))
        kg_df = diff_k_gain[l].reshape(-1)

        hc = _norm_mod(ctx, mcb[:, 1:2], mcb[:, 0:1]).reshape(B * Lc, D)
        kna_c = _proj(hc, wl, o_kna, na_w, "rms128", gain=na_k_gain[l]).reshape(B, Lc, na_w)
        vna_c = _proj(hc, wl, o_vna, na_w, "plain").reshape(B, Lc, na_w)
        kdf_c = _proj(hc, wl, o_kdf, df_w, "rms64", gain=kg_df).reshape(B, Lc, df_w)
        vdf_c = _proj(hc, wl, o_vdf, df_w, "plain").reshape(B, Lc, df_w)

        h = _norm_mod(x, m[:, 1:2], m[:, 0:1]).reshape(B * T, D)
        k_na = _proj(h, wl, o_kna, na_w, "rms128", gain=na_k_gain[l]).reshape(B, T, na_w)
        v_na = _proj(h, wl, o_vna, na_w, "plain").reshape(B, T, na_w)
        k_df = _proj(h, wl, o_kdf, df_w, "rms64", gain=kg_df, cos=cos, sin=sin).reshape(B, T, df_w)
        v_df = _proj(h, wl, o_vdf, df_w, "plain").reshape(B, T, df_w)
        q_na = _proj(h, wl, o_qna, na_w, "rms128", gain=na_q_gain[l]).reshape(B, T, na_w)
        q_df = _proj(h, wl, o_qdf, df_w, "rms64", gain=qg_df, cos=cos, sin=sin).reshape(B, T, df_w)
        u = _proj(h, wl, o_cv, cch, "glu", col0_b=o_cg).reshape(B, T, cch)

        o_na = _na_attn(q_na, k_na, v_na, kna_c, vna_c, na_bias[l])
        o_df = _diff_attn(q_df, kdf_c, vdf_c, k_df, v_df, lam, diff_out_gain[l], post)
        o_cv_ = _conv(u, conv_dw[l], conv_db[l], conv_ln_g[l], conv_ln_b[l], pw)
        x = _outproj(o_cv_, o_na, o_df, wo, x, m[:, 2:3])
        x = _ec_moe(x, m[:, 5:6], m[:, 4:5], m[:, 3:4], w_router[l], wg, wu, wd)

        if update_ctx:
            qna_c = _proj(hc, wl, o_qna, na_w, "rms128", gain=na_q_gain[l]).reshape(B, Lc, na_w)
            qdf_c = _proj(hc, wl, o_qdf, df_w, "rms64", gain=qg_df).reshape(B, Lc, df_w)
            u_c = _proj(hc, wl, o_cv, cch, "glu", col0_b=o_cg).reshape(B, Lc, cch)
            o_na_c = _dense_attn(qna_c, kna_c, vna_c)
            o_df_c = _diff_attn(qdf_c, kdf_c, vdf_c, None, None, lam, diff_out_gain[l], post)
            o_cv_c = _conv(u_c, conv_dw[l], conv_db[l], conv_ln_g[l], conv_ln_b[l], pw)
            ctx = _outproj(o_cv_c, o_na_c, o_df_c, wo, ctx, mcb[:, 2:3])
            ctx = _ec_moe(ctx, mcb[:, 5:6], mcb[:, 4:5], mcb[:, 3:4], w_router[l], wg, wu, wd)
    return x
```

```python
import functools
import math

import numpy as np
import jax
import jax.numpy as jnp
from jax import lax
from jax.experimental import pallas as pl
from jax.experimental.pallas import tpu as pltpu

GRID_W = 64
HEAD_DIM = 128
WIN_R = 8
WIN_C = 16
DIFF_DIM = HEAD_DIM // 2
ROPE_BASE = 10000.0
EC_CAPACITY = 2
N_MOD = 6
EPS = 1e-6
LOG2E = math.log2(math.e)
LANES = 128
NA_QROWS = 4
NEG = -1e30
VMEM_LIMIT = 56 * 1024 * 1024

F32 = jnp.float32
BF16 = jnp.bfloat16


def _cparams(sem):
    return pltpu.CompilerParams(dimension_semantics=sem, vmem_limit_bytes=VMEM_LIMIT)


def _pick(n, cap, mult=LANES):
    best = None
    for t in range(mult, min(n, cap) + 1, mult):
        if n % t == 0:
            best = t
    assert best is not None, (n, cap, mult)
    return best


def _dot(a, b):
    return jnp.dot(a, b, preferred_element_type=F32)


def _dot_nt(a, b):
    return lax.dot_general(a, b, (((1,), (1,)), ((), ())), preferred_element_type=F32)


def _lane_fold(op, arrays):
    tiles = [a[:, t * LANES:(t + 1) * LANES] for a in arrays for t in range(a.shape[1] // LANES)]
    return functools.reduce(op, tiles)


def _lane_tile(x, k):
    return jnp.concatenate([x] * k, axis=1)


def _norm_kernel(x_ref, sc_ref, sh_ref, o_ref):
    x = x_ref[...]
    ms = jnp.mean(x * x, axis=-1, keepdims=True)
    h = x * lax.rsqrt(ms + EPS) * (1.0 + sc_ref[...]) + sh_ref[...]
    o_ref[...] = h.astype(o_ref.dtype)


def _norm_router_kernel(x_ref, sc_ref, sh_ref, wr_ref, o_ref, lg_ref):
    x = x_ref[...]
    ms = jnp.mean(x * x, axis=-1, keepdims=True)
    h = (x * lax.rsqrt(ms + EPS) * (1.0 + sc_ref[...]) + sh_ref[...]).astype(o_ref.dtype)
    o_ref[...] = h
    lg_ref[...] = _dot(h, wr_ref[...])


def _norm_mod(x, scale, shift, w_router=None):
    B, L, D = x.shape
    tl = _pick(L, 512, 8)
    grid = (B, L // tl)
    xspec = pl.BlockSpec((None, tl, D), lambda b, i: (b, i, 0))
    mspec = pl.BlockSpec((None, 1, D), lambda b, i: (b, 0, 0))
    if w_router is None:
        return pl.pallas_call(
            _norm_kernel, grid=grid, in_specs=[xspec, mspec, mspec], out_specs=xspec,
            out_shape=jax.ShapeDtypeStruct((B, L, D), BF16),
            compiler_params=_cparams(("parallel", "parallel")), name="norm_mod",
        )(x, scale, shift)
    E = w_router.shape[1]
    wr = jnp.zeros((D, LANES), BF16).at[:, :E].set(w_router.astype(BF16))
    return pl.pallas_call(
        _norm_router_kernel, grid=grid,
        in_specs=[xspec, mspec, mspec, pl.BlockSpec((D, LANES), lambda b, i: (0, 0))],
        out_specs=[xspec, pl.BlockSpec((None, tl, LANES), lambda b, i: (b, i, 0))],
        out_shape=[jax.ShapeDtypeStruct((B, L, D), BF16),
                   jax.ShapeDtypeStruct((B, L, LANES), F32)],
        compiler_params=_cparams(("parallel", "parallel")), name="norm_router",
    )(x, scale, shift, wr)


PROJ_RCH = 256


def _proj_kernel(kind, rope, tn, *refs):
    if kind == "glu":
        h_ref, w_ref, w2_ref, o_ref = refs
        h = h_ref[...]
        val = _dot(h, w_ref[...])
        gate = _dot(h, w2_ref[...])
        o_ref[...] = val * jax.nn.sigmoid(gate)
        return
    if kind == "plain":
        h_ref, w_ref, o_ref = refs
        o_ref[...] = _dot(h_ref[...], w_ref[...]).astype(o_ref.dtype)
        return
    if rope:
        h_ref, w_ref, g_ref, cos_ref, sin_ref, o_ref = refs
    else:
        h_ref, w_ref, g_ref, o_ref = refs
    tm = h_ref.shape[0]
    rch = min(tm, PROJ_RCH)
    for r0 in range(0, tm, rch):
        acc = _dot(h_ref[r0:r0 + rch, :], w_ref[...])
        for s in range(tn // LANES):
            t = acc[:, s * LANES:(s + 1) * LANES]
            sq = t * t
            lane = lax.broadcasted_iota(jnp.int32, t.shape, 1)
            if kind == "rms128":
                inv = lax.rsqrt(jnp.mean(sq, axis=-1, keepdims=True) + EPS)
            else:
                lo = jnp.sum(jnp.where(lane < DIFF_DIM, sq, 0.0), axis=-1, keepdims=True)
                tot = jnp.sum(sq, axis=-1, keepdims=True)
                inv = jnp.where(lane < DIFF_DIM,
                                lax.rsqrt(lo * (1.0 / DIFF_DIM) + EPS),
                                lax.rsqrt((tot - lo) * (1.0 / DIFF_DIM) + EPS))
            t = t * inv * g_ref[...]
            if rope:
                up = pltpu.roll(t, LANES - 16, axis=1)
                dn = pltpu.roll(t, 16, axis=1)
                sw = jnp.where((lane % 32) < 16, up, dn)
                t = t * cos_ref[r0:r0 + rch, :] + sw * sin_ref[r0:r0 + rch, :]
            o_ref[r0:r0 + rch, s * LANES:(s + 1) * LANES] = t.astype(o_ref.dtype)


def _proj(h, w, col0, ncols, kind, *, gain=None, cos=None, sin=None, col0_b=None):
    M, D = h.shape
    tm = _pick(M, 1024, 8)
    tn = _pick(ncols, 512)
    assert col0 % tn == 0
    grid = (M // tm, ncols // tn)
    hspec = pl.BlockSpec((tm, D), lambda i, j: (i, 0))
    c0 = col0 // tn
    wspec = pl.BlockSpec((D, tn), lambda i, j: (0, c0 + j))
    ospec = pl.BlockSpec((tm, tn), lambda i, j: (i, j))
    rope = cos is not None
    ins, specs = [h, w], [hspec, wspec]
    odt = BF16
    if kind == "glu":
        assert col0_b % tn == 0
        c1 = col0_b // tn
        ins.append(w)
        specs.append(pl.BlockSpec((D, tn), lambda i, j: (0, c1 + j)))
        odt = F32
    elif kind != "plain":
        ins.append(gain.reshape(1, LANES).astype(F32))
        specs.append(pl.BlockSpec((1, LANES), lambda i, j: (0, 0)))
        if rope:
            nt = cos.shape[0] // tm
            tspec = pl.BlockSpec((tm, LANES), lambda i, j: (i % nt, 0))
            ins += [cos, sin]
            specs += [tspec, tspec]
    return pl.pallas_call(
        functools.partial(_proj_kernel, kind, rope, tn), grid=grid,
        in_specs=specs, out_specs=ospec,
        out_shape=jax.ShapeDtypeStruct((M, ncols), odt),
        compiler_params=_cparams(("parallel", "arbitrary")), name="proj_" + kind,
    )(*ins)


def _na_bias(rpb):
    L, H = rpb.shape[:2]
    qc = np.arange(GRID_W)
    kc = np.arange(GRID_W)
    cs = np.clip(qc - WIN_C // 2, 0, GRID_W - WIN_C)
    col_ok = (kc[None, :] >= cs[:, None]) & (kc[None, :] < cs[:, None] + WIN_C)
    dc = kc[None, :] - qc[:, None] + WIN_C - 1
    onehot = (dc[None] == np.arange(2 * WIN_C - 1)[:, None, None]) & col_ok[None]
    tq = jnp.einsum("lhrc,cqk->lhqrk", rpb.astype(F32), jnp.asarray(onehot, F32),
                    precision=lax.Precision.HIGHEST)
    tq = jnp.where(jnp.asarray(col_ok)[:, None, :], tq, NEG)
    nkr = 3 * NA_QROWS

    def neg(n):
        return jnp.full((L, H, GRID_W, n, GRID_W), NEG, F32)

    tabs = []
    for typ in range(3):
        rows_q = []
        for qr in range(NA_QROWS):
            if typ == 0:
                k0, d0 = 0, WIN_R - 1 - qr
            elif typ == 1:
                k0, d0 = qr, WIN_R // 2 - 1
            else:
                k0, d0 = nkr - WIN_R, nkr - 2 * NA_QROWS - 1 - qr
            parts = [tq[:, :, :, d0:d0 + WIN_R, :]]
            if k0:
                parts.insert(0, neg(k0))
            if nkr - k0 - WIN_R:
                parts.append(neg(nkr - k0 - WIN_R))
            rows_q.append(jnp.concatenate(parts, axis=3).reshape(L, H, GRID_W, nkr * GRID_W))
        tabs.append(jnp.concatenate(rows_q, axis=2))
    return jnp.stack(tabs, axis=1)


def _na_kernel(hps, q_ref, k0_ref, k1_ref, k2_ref, v0_ref, v1_ref, v2_ref,
               kc_ref, vc_ref, b_ref, o_ref):
    nq = q_ref.shape[0]
    krefs = (k0_ref, k1_ref, k2_ref)
    vrefs = (v0_ref, v1_ref, v2_ref)
    for hh in range(hps):
        ls = slice(hh * HEAD_DIM, (hh + 1) * HEAD_DIM)
        q = q_ref[:, ls]
        s_loc = [_dot_nt(q, krefs[d][:, ls]) + b_ref[hh, :, d * nq:(d + 1) * nq] for d in range(3)]
        s_ctx = _dot_nt(q, kc_ref[:, ls])
        m = jnp.max(_lane_fold(jnp.maximum, s_loc + [s_ctx]), axis=-1, keepdims=True)
        p_ctx = jnp.exp2(s_ctx - m)
        p_loc = [jnp.exp2(s - m) for s in s_loc]
        l = jnp.sum(_lane_fold(jnp.add, p_loc + [p_ctx]), axis=-1, keepdims=True)
        o = _dot(p_ctx.astype(BF16), vc_ref[:, ls])
        for d in range(3):
            o = o + _dot(p_loc[d].astype(BF16), vrefs[d][:, ls])
        o_ref[:, ls] = (o / l).astype(o_ref.dtype)


def _na_attn(q, k, v, kc, vc, bias):
    B, T, W = q.shape
    Lc = kc.shape[1]
    H = W // HEAD_DIM
    hps = 2 if H % 2 == 0 else 1
    hw = hps * HEAD_DIM
    nq = NA_QROWS * GRID_W
    nblk = T // nq
    assert T % nq == 0 and nblk >= 4
    grid = (H // hps, B, nblk)

    def kmap(d):
        return lambda h, b, i: (b, jnp.clip(i - 1, 0, nblk - 3) + d, h)

    qspec = pl.BlockSpec((None, nq, hw), lambda h, b, i: (b, i, h))
    kspecs = [pl.BlockSpec((None, nq, hw), kmap(d)) for d in range(3)]
    cspec = pl.BlockSpec((None, Lc, hw), lambda h, b, i: (b, 0, h))
    bspec = pl.BlockSpec(
        (None, hps, nq, 3 * nq),
        lambda h, b, i: (jnp.where(i == 0, 0, jnp.where(i == nblk - 1, 2, 1)), h, 0, 0))
    return pl.pallas_call(
        functools.partial(_na_kernel, hps), grid=grid,
        in_specs=[qspec] + kspecs + kspecs + [cspec, cspec, bspec], out_specs=qspec,
        out_shape=jax.ShapeDtypeStruct((B, T, W), BF16),
        compiler_params=_cparams(("parallel", "parallel", "arbitrary")), name="na_attn",
    )(q, k, k, k, v, v, v, kc, vc, bias)


def _dense_kernel(q_ref, k_ref, v_ref, o_ref):
    s = _dot_nt(q_ref[...], k_ref[...])
    m = jnp.max(s, axis=-1, keepdims=True)
    p = jnp.exp2(s - m)
    l = jnp.sum(p, axis=-1, keepdims=True)
    o_ref[...] = (_dot(p.astype(BF16), v_ref[...]) / l).astype(o_ref.dtype)


def _dense_attn(q, k, v):
    B, L, W = q.shape
    H = W // HEAD_DIM
    spec = pl.BlockSpec((None, L, HEAD_DIM), lambda b, h: (b, 0, h))
    return pl.pallas_call(
        _dense_kernel, grid=(B, H),
        in_specs=[spec, spec, spec], out_specs=spec,
        out_shape=jax.ShapeDtypeStruct((B, L, W), BF16),
        compiler_params=_cparams(("parallel", "parallel")), name="dense_attn",
    )(q, k, v)


def _diff_kernel(tk, n_lat, post_scale, lam_ref, q_ref, kc_ref, vc_ref, *refs):
    if n_lat:
        k_ref, v_ref, g_ref, o_ref, qs_sc, m_sc, acc_sc, vcx_sc, vx_sc, sa_sc, sb_sc = refs
    else:
        g_ref, o_ref, qs_sc, m_sc, acc_sc, vcx_sc = refs
    q = q_ref[...]
    tq = q.shape[0]

    def with_ones(dst, src):
        lane = lax.broadcasted_iota(jnp.int32, src.shape, 1)
        dst[:, 0:HEAD_DIM] = src[...]
        dst[:, HEAD_DIM:] = jnp.where(lane == 0, 1.0, 0.0).astype(BF16)

    @pl.when(pl.program_id(2) == 0)
    def _():
        with_ones(vcx_sc, vc_ref)
        if n_lat:
            with_ones(vx_sc, v_ref)

    lane = lax.broadcasted_iota(jnp.int32, q.shape, 1)
    zero = jnp.zeros_like(q)
    qs_sc[0:tq, :] = jnp.where(lane < DIFF_DIM, q, zero)
    qs_sc[tq:, :] = jnp.where(lane >= DIFF_DIM, q, zero)
    m_sc[...] = jnp.full(m_sc.shape, -jnp.inf, F32)
    acc_sc[...] = jnp.zeros(acc_sc.shape, F32)

    def softmax_pv(s, vb):
        m_old = m_sc[...]
        m_new = jnp.maximum(m_old, jnp.max(s, axis=-1, keepdims=True))
        alpha = jnp.exp2(m_old - m_new)
        p = jnp.exp2((s - _lane_tile(m_new, s.shape[1] // LANES)).astype(BF16))
        acc_sc[...] = _lane_tile(alpha, 2) * acc_sc[...] + _dot(p, vb)
        m_sc[...] = m_new

    s_ctx = _dot_nt(qs_sc[...], kc_ref[...])
    if n_lat:
        sa_sc[...] = _dot_nt(qs_sc[...], k_ref[pl.ds(0, tk), :])
    softmax_pv(s_ctx, vcx_sc[...])
    if n_lat:

        def body(c2, carry):
            c = 2 * c2
            off0 = pl.multiple_of(c * tk, tk)
            off1 = pl.multiple_of((c + 1) * tk, tk)
            off2 = pl.multiple_of(jnp.minimum(c + 2, n_lat - 1) * tk, tk)
            sb_sc[...] = _dot_nt(qs_sc[...], k_ref[pl.ds(off1, tk), :])
            softmax_pv(sa_sc[...], vx_sc[pl.ds(off0, tk), :])
            sa_sc[...] = _dot_nt(qs_sc[...], k_ref[pl.ds(off2, tk), :])
            softmax_pv(sb_sc[...], vx_sc[pl.ds(off1, tk), :])
            return carry

        lax.fori_loop(0, n_lat // 2, body, 0)
    acc = acc_sc[...]
    o = acc[:, :HEAD_DIM] / acc[:, HEAD_DIM:HEAD_DIM + 1]
    o = o[:tq] - lam_ref[0, 0] * o[tq:]
    o = o * lax.rsqrt(jnp.mean(o * o, axis=-1, keepdims=True) + EPS)
    o_ref[...] = (o * g_ref[...] * post_scale).astype(o_ref.dtype)


def _diff_attn(q, kc, vc, k, v, lam, gain, post_scale):
    B, L, W = q.shape
    Lc = kc.shape[1]
    H = W // HEAD_DIM
    tq = _pick(L, 512, 8)
    grid = (B, H, L // tq)
    qspec = pl.BlockSpec((None, tq, HEAD_DIM), lambda b, h, i: (b, i, h))
    cspec = pl.BlockSpec((None, Lc, HEAD_DIM), lambda b, h, i: (b, 0, h))
    ins = [lam.reshape(1, 1).astype(F32), q, kc, vc]
    specs = [pl.BlockSpec(memory_space=pltpu.SMEM), qspec, cspec, cspec]
    scratch = [pltpu.VMEM((2 * tq, HEAD_DIM), BF16), pltpu.VMEM((2 * tq, LANES), F32),
               pltpu.VMEM((2 * tq, 2 * HEAD_DIM), F32), pltpu.VMEM((Lc, 2 * HEAD_DIM), BF16)]
    tk, n_lat = 0, 0
    if k is not None:
        T = k.shape[1]
        tk = _pick(T, 1024, 8)
        n_lat = T // tk
        assert n_lat % 2 == 0
        lspec = pl.BlockSpec((None, T, HEAD_DIM), lambda b, h, i: (b, 0, h))
        ins += [k, v]
        specs += [lspec, lspec]
        scratch += [pltpu.VMEM((T, 2 * HEAD_DIM), BF16),
                    pltpu.VMEM((2 * tq, tk), F32), pltpu.VMEM((2 * tq, tk), F32)]
    ins.append(gain.reshape(1, HEAD_DIM).astype(F32))
    specs.append(pl.BlockSpec((1, HEAD_DIM), lambda b, h, i: (0, 0)))
    return pl.pallas_call(
        functools.partial(_diff_kernel, tk, n_lat, post_scale), grid=grid,
        in_specs=specs, out_specs=qspec,
        out_shape=jax.ShapeDtypeStruct((B, L, W), BF16), scratch_shapes=scratch,
        compiler_params=_cparams(("parallel", "parallel", "arbitrary")), name="diff_attn",
    )(*ins)


CONV_HALO = 16
CONV_RC = 32


def _conv_kernel(cw, nt, prev_ref, cur_ref, next_ref, dw_ref, db_ref, g_ref, b_ref,
                 pw_ref, o_ref, pad_ref, y_ref):
    i = pl.program_id(1)
    tt, ch = cur_ref.shape
    half = cw // 2
    pad_ref[0:CONV_HALO, :] = jnp.where(i == 0, 0.0, prev_ref[...])
    pad_ref[CONV_HALO:CONV_HALO + tt, :] = cur_ref[...]
    pad_ref[CONV_HALO + tt:, :] = jnp.where(i == nt - 1, 0.0, next_ref[...])
    base = CONV_HALO - half
    for r0 in range(0, tt, CONV_RC):
        for c0 in range(0, ch, LANES):
            acc = jnp.zeros((CONV_RC, LANES), F32)
            for j in range(cw):
                acc = acc + dw_ref[j:j + 1, c0:c0 + LANES] * \
                    pad_ref[r0 + base + j:r0 + base + j + CONV_RC, c0:c0 + LANES]
            y_ref[r0:r0 + CONV_RC, c0:c0 + LANES] = acc
    y = y_ref[...] + db_ref[...]
    mu = jnp.mean(y, axis=-1, keepdims=True)
    yc = y - mu
    var = jnp.mean(yc * yc, axis=-1, keepdims=True)
    z = yc * lax.rsqrt(var + EPS) * g_ref[...] + b_ref[...]
    z = z * jax.nn.sigmoid(z)
    o_ref[...] = _dot(z.astype(BF16), pw_ref[...]).astype(o_ref.dtype)


def _conv(u, dw, db, ln_g, ln_b, pw):
    B, L, C = u.shape
    cw = dw.shape[0]
    assert cw // 2 <= CONV_HALO
    tt = _pick(L, 256, CONV_RC)
    nt = L // tt
    hb = tt // CONV_HALO
    nh = L // CONV_HALO
    cur = pl.BlockSpec((None, tt, C), lambda b, i: (b, i, 0))
    prev = pl.BlockSpec((None, CONV_HALO, C), lambda b, i: (b, jnp.maximum(i * hb - 1, 0), 0))
    nxt = pl.BlockSpec((None, CONV_HALO, C), lambda b, i: (b, jnp.minimum((i + 1) * hb, nh - 1), 0))
    vec = pl.BlockSpec((1, C), lambda b, i: (0, 0))
    return pl.pallas_call(
        functools.partial(_conv_kernel, cw, nt), grid=(B, nt),
        in_specs=[prev, cur, nxt, pl.BlockSpec((cw, C), lambda b, i: (0, 0)), vec, vec, vec,
                  pl.BlockSpec((C, C), lambda b, i: (0, 0))],
        out_specs=cur, out_shape=jax.ShapeDtypeStruct((B, L, C), BF16),
        scratch_shapes=[pltpu.VMEM((tt + 2 * CONV_HALO, C), F32), pltpu.VMEM((tt, C), F32)],
        compiler_params=_cparams(("parallel", "parallel")), name="conv",
    )(u, u, u, dw.astype(F32), db.reshape(1, C).astype(F32), ln_g.reshape(1, C).astype(F32),
      ln_b.reshape(1, C).astype(F32), pw)


def _outproj_kernel(ka, kb, a_ref, b_ref, c_ref, w_ref, x_ref, g_ref, o_ref):
    acc = _dot(a_ref[...], w_ref[0:ka, :])
    acc = acc + _dot(b_ref[...], w_ref[ka:ka + kb, :])
    acc = acc + _dot(c_ref[...], w_ref[ka + kb:, :])
    o_ref[...] = x_ref[...] + g_ref[...] * acc


def _outproj(a, b, c, w, x, gate):
    B, L, D = x.shape
    ka, kb, kc = a.shape[2], b.shape[2], c.shape[2]
    K = ka + kb + kc
    tm = _pick(L, 1024, 8)
    tn = _pick(D, 512)
    grid = (B, L // tm, D // tn)

    def aspec(k):
        return pl.BlockSpec((None, tm, k), lambda bb, i, j: (bb, i, 0))

    xspec = pl.BlockSpec((None, tm, tn), lambda bb, i, j: (bb, i, j))
    return pl.pallas_call(
        functools.partial(_outproj_kernel, ka, kb), grid=grid,
        in_specs=[aspec(ka), aspec(kb), aspec(kc),
                  pl.BlockSpec((K, tn), lambda bb, i, j: (0, j)), xspec,
                  pl.BlockSpec((None, 1, tn), lambda bb, i, j: (bb, 0, j))],
        out_specs=xspec, out_shape=jax.ShapeDtypeStruct((B, L, D), F32),
        compiler_params=_cparams(("parallel", "parallel", "arbitrary")), name="outproj",
    )(a, b, c, w, x, gate)


def _ffn_kernel(xs_ref, wg_ref, wu_ref, wd_ref, g_ref, o_ref):
    xs = xs_ref[...]
    a = _dot(xs, wg_ref[...])
    u = _dot(xs, wu_ref[...])
    hmid = (a * jax.nn.sigmoid(a) * u).astype(BF16)
    o_ref[...] = (_dot(hmid, wd_ref[...]) * g_ref[...]).astype(o_ref.dtype)


def _experts(xs, wg, wu, wd, g):
    B, E, C, D = xs.shape
    Fd = wg.shape[2]
    tc = _pick(C, 256, 8)
    grid = (E, B, C // tc)
    return pl.pallas_call(
        _ffn_kernel, grid=grid,
        in_specs=[pl.BlockSpec((None, None, tc, D), lambda e, b, i: (b, e, i, 0)),
                  pl.BlockSpec((None, D, Fd), lambda e, b, i: (e, 0, 0)),
                  pl.BlockSpec((None, D, Fd), lambda e, b, i: (e, 0, 0)),
                  pl.BlockSpec((None, Fd, D), lambda e, b, i: (e, 0, 0)),
                  pl.BlockSpec((None, None, tc, 1), lambda e, b, i: (b, e, i, 0))],
        out_specs=pl.BlockSpec((None, None, tc, D), lambda e, b, i: (b, e, i, 0)),
        out_shape=jax.ShapeDtypeStruct((B, E, C, D), BF16),
        compiler_params=_cparams(("parallel", "parallel", "arbitrary")), name="experts",
    )(xs, wg, wu, wd, g)


MOE_TT = 256
MOE_RB = 256


def _combine_kernel(tt, tile_ref, blk_ref, flag_ref, tok_ref, y_ref, x_ref, g_ref, o_ref, acc_sc):
    b, j = pl.program_id(0), pl.program_id(1)
    fl = flag_ref[b, j]

    @pl.when((fl & 1) != 0)
    def _():
        acc_sc[...] = jnp.zeros(acc_sc.shape, F32)

    @pl.when((fl & 4) != 0)
    def _():
        rb = y_ref.shape[0]
        toks = tile_ref[b, j] * tt + lax.broadcasted_iota(jnp.int32, (tt, rb), 0)
        sel = jnp.where(tok_ref[...] == toks, 1.0, 0.0).astype(BF16)
        acc_sc[...] += _dot(sel, y_ref[...])

    @pl.when((fl & 2) != 0)
    def _():
        o_ref[...] = x_ref[...] + g_ref[...] * acc_sc[...]


def _combine_plan(tok_sorted, L, tt, rb):
    B, R = tok_sorted.shape
    ntile, nblk = L // tt, R // rb
    bounds = jnp.arange(ntile + 1, dtype=jnp.int32) * tt
    cut = jax.vmap(lambda ts: jnp.searchsorted(ts, bounds, side="left"))(tok_sorted).astype(jnp.int32)
    lo, hi = cut[:, :-1], cut[:, 1:]
    kfirst = jnp.minimum(lo // rb, nblk - 1)
    klast = jnp.maximum(kfirst, (hi - 1) // rb)
    nb = klast - kfirst + 1
    ends = jnp.cumsum(nb, axis=1)
    starts = ends - nb
    ns = ntile + nblk
    j = jnp.arange(ns, dtype=jnp.int32)
    tile = jax.vmap(lambda st: jnp.searchsorted(st, j, side="right"))(starts).astype(jnp.int32) - 1
    tile = jnp.clip(tile, 0, ntile - 1)
    off = j[None, :] - jnp.take_along_axis(starts, tile, axis=1)
    nb_t = jnp.take_along_axis(nb, tile, axis=1)
    valid = j[None, :] < ends[:, -1:]
    blk = jnp.clip(jnp.take_along_axis(kfirst, tile, axis=1) + off, 0, nblk - 1)
    flags = (valid & (off == 0)) * 1 + (valid & (off == nb_t - 1)) * 2 + valid * 4
    return tile, blk.astype(jnp.int32), flags.astype(jnp.int32)


def _combine(y_sorted, tok_sorted, xres, gate):
    B, L, D = xres.shape
    R = y_sorted.shape[1]
    tt = _pick(L, MOE_TT, 8)
    rb = _pick(R, MOE_RB)
    tile, blk, flags = _combine_plan(tok_sorted, L, tt, rb)
    ns = tile.shape[1]
    tok4 = tok_sorted.reshape(B, R // rb, 1, rb)
    grid_spec = pltpu.PrefetchScalarGridSpec(
        num_scalar_prefetch=3, grid=(B, ns),
        in_specs=[
            pl.BlockSpec((None, None, 1, rb), lambda b, j, t, k, f: (b, k[b, j], 0, 0)),
            pl.BlockSpec((None, rb, D), lambda b, j, t, k, f: (b, k[b, j], 0)),
            pl.BlockSpec((None, tt, D), lambda b, j, t, k, f: (b, t[b, j], 0)),
            pl.BlockSpec((None, 1, D), lambda b, j, t, k, f: (b, 0, 0)),
        ],
        out_specs=pl.BlockSpec((None, tt, D), lambda b, j, t, k, f: (b, t[b, j], 0)),
        scratch_shapes=[pltpu.VMEM((tt, D), F32)])
    return pl.pallas_call(
        functools.partial(_combine_kernel, tt), grid_spec=grid_spec,
        out_shape=jax.ShapeDtypeStruct((B, L, D), F32),
        compiler_params=_cparams(("parallel", "arbitrary")), name="moe_combine",
    )(tile, blk, flags, tok4, y_sorted, xres, gate)


def _ec_moe(xres, gate, scale, shift, w_router, wg, wu, wd):
    B, L, D = xres.shape
    E = w_router.shape[1]
    cap = max(1, EC_CAPACITY * L // E)
    h2, logits = _norm_mod(xres, scale, shift, w_router)
    aff = jax.nn.softmax(logits[..., :E], axis=-1)
    g, idx = lax.top_k(jnp.swapaxes(aff, 1, 2), cap)
    bidx = jnp.arange(B)[:, None, None]
    xs = h2[bidx, idx]
    y = _experts(xs, wg, wu, wd, g[..., None]).reshape(B, E * cap, D)
    flat_tok = idx.reshape(B, E * cap).astype(jnp.int32)
    order = jnp.argsort(flat_tok, axis=1)
    tok_sorted = jnp.take_along_axis(flat_tok, order, axis=1)
    y_sorted = jnp.take_along_axis(y, order[:, :, None], axis=1)
    return _combine(y_sorted, tok_sorted, xres, gate)


def _ada(cvec, down, up, bias, n, D):
    hi = lax.Precision.HIGHEST
    z = jnp.dot(jax.nn.silu(cvec), down, precision=hi)
    m = jnp.dot(z, up[:, :n * D], precision=hi) + bias[:n * D]
    return m.reshape(cvec.shape[:-1] + (n, D))


def _rope_tables(T):
    nf = DIFF_DIM // 4
    inv = ROPE_BASE ** (-jnp.arange(nf, dtype=F32) / nf)
    t_idx = jnp.arange(T)
    ang_r = (t_idx // GRID_W).astype(F32)[:, None] * inv
    ang_c = (t_idx % GRID_W).astype(F32)[:, None] * inv
    cr, sr, cc, sc = jnp.cos(ang_r), jnp.sin(ang_r), jnp.cos(ang_c), jnp.sin(ang_c)
    cos = jnp.concatenate([cr, cr, cc, cc] * 2, axis=-1)
    sin = jnp.concatenate([-sr, sr, -sc, sc] * 2, axis=-1)
    return cos, sin


def kernel(x, c, ctx, c_ctx, ada_down, ada_up, ada_bias, w_in, conv_dw, conv_db, conv_ln_g,
           conv_ln_b, conv_pw, na_q_gain, na_k_gain, na_rpb, diff_q_gain, diff_k_gain, diff_lam,
           diff_out_gain, w_out, w_router, w_gate, w_up, w_down):
    B, T, D = x.shape
    Lc = ctx.shape[1]
    depth = w_in.shape[0]
    na_w = 3 * D // 8
    df_w = 3 * D // 8
    cch = D // 4
    o_kna, o_vna, o_kdf, o_vdf = 0, na_w, 2 * na_w, 2 * na_w + df_w
    o_qna = 2 * na_w + 2 * df_w
    o_qdf = o_qna + na_w
    o_cv = o_qdf + df_w
    o_cg = o_cv + cch
    cos, sin = _rope_tables(T)
    na_bias = _na_bias(na_rpb.astype(F32) * LOG2E)

    for l in range(depth):
        update_ctx = l < depth - 1
        lam_init = 0.8 - 0.6 * math.exp(-0.3 * l)
        lv = diff_lam[l].astype(F32)
        lam = jnp.exp(jnp.sum(lv[0] * lv[1])) - jnp.exp(jnp.sum(lv[2] * lv[3])) + lam_init
        post = 1.0 - lam_init
        m = _ada(c, ada_down[l], ada_up[l], ada_bias[l], N_MOD, D)
        mc = _ada(c_ctx, ada_down[l], ada_up[l], ada_bias[l], N_MOD if update_ctx else 2, D)
        mcb = jnp.broadcast_to(mc[None], (B,) + mc.shape)
        wl = w_in[l].astype(BF16)
        wo = w_out[l].astype(BF16)
        pw = conv_pw[l].astype(BF16)
        wg, wu, wd = w_gate[l].astype(BF16), w_up[l].astype(BF16), w_down[l].astype(BF16)
        qg_df = diff_q_gain[l].reshape(-1) * (DIFF_DIM ** -0.5 * LOG2E)
        qg_na = na_q_gain[l] * (HEAD_DIM ** -0.5 * LOG2E)
        kg_df = diff_k_gain[l].reshape(-1)

        hc = _norm_mod(ctx, mcb[:, 1:2], mcb[:, 0:1]).reshape(B * Lc, D)
        kna_c = _proj(hc, wl, o_kna, na_w, "rms128", gain=na_k_gain[l]).reshape(B, Lc, na_w)
        vna_c = _proj(hc, wl, o_vna, na_w, "plain").reshape(B, Lc, na_w)
        kdf_c = _proj(hc, wl, o_kdf, df_w, "rms64", gain=kg_df).reshape(B, Lc, df_w)
        vdf_c = _proj(hc, wl, o_vdf, df_w, "plain").reshape(B, Lc, df_w)

        h = _norm_mod(x, m[:, 1:2], m[:, 0:1]).reshape(B * T, D)
        k_na = _proj(h, wl, o_kna, na_w, "rms128", gain=na_k_gain[l]).reshape(B, T, na_w)
        v_na = _proj(h, wl, o_vna, na_w, "plain").reshape(B, T, na_w)
        k_df = _proj(h, wl, o_kdf, df_w, "rms64", gain=kg_df, cos=cos, sin=sin).reshape(B, T, df_w)
        v_df = _proj(h, wl, o_vdf, df_w, "plain").reshape(B, T, df_w)
        q_na = _proj(h, wl, o_qna, na_w, "rms128", gain=qg_na).reshape(B, T, na_w)
        q_df = _proj(h, wl, o_qdf, df_w, "rms64", gain=qg_df, cos=cos, sin=sin).reshape(B, T, df_w)
        u = _proj(h, wl, o_cv, cch, "glu", col0_b=o_cg).reshape(B, T, cch)

        o_na = _na_attn(q_na, k_na, v_na, kna_c, vna_c, na_bias[l])
        o_df = _diff_attn(q_df, kdf_c, vdf_c, k_df, v_df, lam, diff_out_gain[l], post)
        o_cv_ = _conv(u, conv_dw[l], conv_db[l], conv_ln_g[l], conv_ln_b[l], pw)
        x = _outproj(o_cv_, o_na, o_df, wo, x, m[:, 2:3])
        x = _ec_moe(x, m[:, 5:6], m[:, 4:5], m[:, 3:4], w_router[l], wg, wu, wd)

        if update_ctx:
            qna_c = _proj(hc, wl, o_qna, na_w, "rms128", gain=qg_na).reshape(B, Lc, na_w)
            qdf_c = _proj(hc, wl, o_qdf, df_w, "rms64", gain=qg_df).reshape(B, Lc, df_w)
            u_c = _proj(hc, wl, o_cv, cch, "glu", col0_b=o_cg).reshape(B, Lc, cch)
            o_na_c = _dense_attn(qna_c, kna_c, vna_c)
            o_df_c = _diff_attn(qdf_c, kdf_c, vdf_c, None, None, lam, diff_out_gain[l], post)
            o_cv_c = _conv(u_c, conv_dw[l], conv_db[l], conv_ln_g[l], conv_ln_b[l], pw)
            ctx = _outproj(o_cv_c, o_na_c, o_df_c, wo, ctx, mcb[:, 2:3])
            ctx = _ec_moe(ctx, mcb[:, 5:6], mcb[:, 4:5], mcb[:, 3:4], w_router[l], wg, wu, wd)
    return x
```

```python
import functools
import math

import numpy as np
import jax
import jax.numpy as jnp
from jax import lax
from jax.experimental import pallas as pl
from jax.experimental.pallas import tpu as pltpu

GRID_W = 64
HEAD_DIM = 128
WIN_R = 8
WIN_C = 16
DIFF_DIM = HEAD_DIM // 2
ROPE_BASE = 10000.0
EC_CAPACITY = 2
N_MOD = 6
EPS = 1e-6
LOG2E = math.log2(math.e)
LANES = 128
SUBLANES = 8
NA_QROWS = 4
NEG = -1e30
VMEM_LIMIT = 56 * 1024 * 1024

F32 = jnp.float32
BF16 = jnp.bfloat16


def _cparams(sem):
    return pltpu.CompilerParams(dimension_semantics=sem, vmem_limit_bytes=VMEM_LIMIT)


def _pick(n, cap, mult=LANES):
    best = None
    for t in range(mult, min(n, cap) + 1, mult):
        if n % t == 0:
            best = t
    assert best is not None, (n, cap, mult)
    return best


def _dot(a, b):
    return jnp.dot(a, b, preferred_element_type=F32)


def _dot_nt(a, b):
    return lax.dot_general(a, b, (((1,), (1,)), ((), ())), preferred_element_type=F32)


def _lane_fold(op, arrays):
    tiles = [a[:, t * LANES:(t + 1) * LANES] for a in arrays for t in range(a.shape[1] // LANES)]
    return functools.reduce(op, tiles)


def _lane_tile(x, k):
    return jnp.concatenate([x] * k, axis=1)


def _norm_kernel(x_ref, sc_ref, sh_ref, o_ref):
    x = x_ref[...]
    ms = jnp.mean(x * x, axis=-1, keepdims=True)
    h = x * lax.rsqrt(ms + EPS) * (1.0 + sc_ref[...]) + sh_ref[...]
    o_ref[...] = h.astype(o_ref.dtype)


def _norm_router_kernel(x_ref, sc_ref, sh_ref, wr_ref, o_ref, lg_ref):
    x = x_ref[...]
    ms = jnp.mean(x * x, axis=-1, keepdims=True)
    h = (x * lax.rsqrt(ms + EPS) * (1.0 + sc_ref[...]) + sh_ref[...]).astype(o_ref.dtype)
    o_ref[...] = h
    lg_ref[...] = _dot(h, wr_ref[...])


def _norm_mod(x, scale, shift, w_router=None):
    B, L, D = x.shape
    tl = _pick(L, 512, 8)
    grid = (B, L // tl)
    xspec = pl.BlockSpec((None, tl, D), lambda b, i: (b, i, 0))
    mspec = pl.BlockSpec((None, 1, D), lambda b, i: (b, 0, 0))
    if w_router is None:
        return pl.pallas_call(
            _norm_kernel, grid=grid, in_specs=[xspec, mspec, mspec], out_specs=xspec,
            out_shape=jax.ShapeDtypeStruct((B, L, D), BF16),
            compiler_params=_cparams(("parallel", "parallel")), name="norm_mod",
        )(x, scale, shift)
    E = w_router.shape[1]
    wr = jnp.zeros((D, LANES), BF16).at[:, :E].set(w_router.astype(BF16))
    return pl.pallas_call(
        _norm_router_kernel, grid=grid,
        in_specs=[xspec, mspec, mspec, pl.BlockSpec((D, LANES), lambda b, i: (0, 0))],
        out_specs=[xspec, pl.BlockSpec((None, tl, LANES), lambda b, i: (b, i, 0))],
        out_shape=[jax.ShapeDtypeStruct((B, L, D), BF16),
                   jax.ShapeDtypeStruct((B, L, LANES), F32)],
        compiler_params=_cparams(("parallel", "parallel")), name="norm_router",
    )(x, scale, shift, wr)


PROJ_RCH = 256


def _proj_kernel(kind, rope, tn, *refs):
    if kind == "glu":
        h_ref, w_ref, w2_ref, o_ref = refs
        h = h_ref[...]
        val = _dot(h, w_ref[...].astype(BF16))
        gate = _dot(h, w2_ref[...].astype(BF16))
        o_ref[...] = val * jax.nn.sigmoid(gate)
        return
    if kind == "plain":
        h_ref, w_ref, o_ref = refs
        o_ref[...] = _dot(h_ref[...], w_ref[...].astype(BF16)).astype(o_ref.dtype)
        return
    if rope:
        h_ref, w_ref, g_ref, cos_ref, sin_ref, o_ref = refs
    else:
        h_ref, w_ref, g_ref, o_ref = refs
    tm = h_ref.shape[0]
    rch = min(tm, PROJ_RCH)
    w = w_ref[...].astype(BF16)
    for r0 in range(0, tm, rch):
        acc = _dot(h_ref[r0:r0 + rch, :], w)
        for s in range(tn // LANES):
            t = acc[:, s * LANES:(s + 1) * LANES]
            sq = t * t
            lane = lax.broadcasted_iota(jnp.int32, t.shape, 1)
            if kind == "rms128":
                inv = lax.rsqrt(jnp.mean(sq, axis=-1, keepdims=True) + EPS)
            else:
                lo = jnp.sum(jnp.where(lane < DIFF_DIM, sq, 0.0), axis=-1, keepdims=True)
                tot = jnp.sum(sq, axis=-1, keepdims=True)
                inv = jnp.where(lane < DIFF_DIM,
                                lax.rsqrt(lo * (1.0 / DIFF_DIM) + EPS),
                                lax.rsqrt((tot - lo) * (1.0 / DIFF_DIM) + EPS))
            t = t * inv * g_ref[...]
            if rope:
                up = pltpu.roll(t, LANES - 16, axis=1)
                dn = pltpu.roll(t, 16, axis=1)
                sw = jnp.where((lane % 32) < 16, up, dn)
                t = t * cos_ref[r0:r0 + rch, :] + sw * sin_ref[r0:r0 + rch, :]
            o_ref[r0:r0 + rch, s * LANES:(s + 1) * LANES] = t.astype(o_ref.dtype)


def _proj(h, w, col0, ncols, kind, *, gain=None, cos=None, sin=None, col0_b=None):
    w, layer = w
    M, D = h.shape
    tm = _pick(M, 1024, 8)
    tn = _pick(ncols, 256 if kind == "glu" else 512)
    assert col0 % tn == 0
    grid = (M // tm, ncols // tn)
    hspec = pl.BlockSpec((tm, D), lambda i, j: (i, 0))
    c0 = col0 // tn
    wspec = pl.BlockSpec((None, D, tn), lambda i, j: (layer, 0, c0 + j))
    ospec = pl.BlockSpec((tm, tn), lambda i, j: (i, j))
    rope = cos is not None
    ins, specs = [h, w], [hspec, wspec]
    odt = BF16
    if kind == "glu":
        assert col0_b % tn == 0
        c1 = col0_b // tn
        ins.append(w)
        specs.append(pl.BlockSpec((None, D, tn), lambda i, j: (layer, 0, c1 + j)))
        odt = F32
    elif kind != "plain":
        ins.append(gain.reshape(1, LANES).astype(F32))
        specs.append(pl.BlockSpec((1, LANES), lambda i, j: (0, 0)))
        if rope:
            nt = cos.shape[0] // tm
            tspec = pl.BlockSpec((tm, LANES), lambda i, j: (i % nt, 0))
            ins += [cos, sin]
            specs += [tspec, tspec]
    return pl.pallas_call(
        functools.partial(_proj_kernel, kind, rope, tn), grid=grid,
        in_specs=specs, out_specs=ospec,
        out_shape=jax.ShapeDtypeStruct((M, ncols), odt),
        compiler_params=_cparams(("parallel", "arbitrary")), name="proj_" + kind,
    )(*ins)


def _na_bias(rpb):
    L, H = rpb.shape[:2]
    qc = np.arange(GRID_W)
    kc = np.arange(GRID_W)
    cs = np.clip(qc - WIN_C // 2, 0, GRID_W - WIN_C)
    col_ok = (kc[None, :] >= cs[:, None]) & (kc[None, :] < cs[:, None] + WIN_C)
    dc = kc[None, :] - qc[:, None] + WIN_C - 1
    onehot = (dc[None] == np.arange(2 * WIN_C - 1)[:, None, None]) & col_ok[None]
    tq = jnp.einsum("lhrc,cqk->lhqrk", rpb.astype(F32), jnp.asarray(onehot, F32),
                    precision=lax.Precision.HIGHEST)
    tq = jnp.where(jnp.asarray(col_ok)[:, None, :], tq, NEG)
    nkr = 3 * NA_QROWS

    def neg(n):
        return jnp.full((L, H, GRID_W, n, GRID_W), NEG, F32)

    tabs = []
    for typ in range(3):
        rows_q = []
        for qr in range(NA_QROWS):
            if typ == 0:
                k0, d0 = 0, WIN_R - 1 - qr
            elif typ == 1:
                k0, d0 = qr, WIN_R // 2 - 1
            else:
                k0, d0 = nkr - WIN_R, nkr - 2 * NA_QROWS - 1 - qr
            parts = [tq[:, :, :, d0:d0 + WIN_R, :]]
            if k0:
                parts.insert(0, neg(k0))
            if nkr - k0 - WIN_R:
                parts.append(neg(nkr - k0 - WIN_R))
            rows_q.append(jnp.concatenate(parts, axis=3).reshape(L, H, GRID_W, nkr * GRID_W))
        tabs.append(jnp.concatenate(rows_q, axis=2))
    return jnp.stack(tabs, axis=1)


def _na_kernel(hps, q_ref, k0_ref, k1_ref, k2_ref, v0_ref, v1_ref, v2_ref,
               kc_ref, vc_ref, b_ref, o_ref):
    nq = q_ref.shape[0]
    krefs = (k0_ref, k1_ref, k2_ref)
    vrefs = (v0_ref, v1_ref, v2_ref)
    for hh in range(hps):
        ls = slice(hh * HEAD_DIM, (hh + 1) * HEAD_DIM)
        q = q_ref[:, ls]
        s_loc = [_dot_nt(q, krefs[d][:, ls]) + b_ref[hh, :, d * nq:(d + 1) * nq] for d in range(3)]
        s_ctx = _dot_nt(q, kc_ref[:, ls])
        m = jnp.max(_lane_fold(jnp.maximum, s_loc + [s_ctx]), axis=-1, keepdims=True)
        p_ctx = jnp.exp2(s_ctx - m)
        p_loc = [jnp.exp2(s - m) for s in s_loc]
        l = jnp.sum(_lane_fold(jnp.add, p_loc + [p_ctx]), axis=-1, keepdims=True)
        o = _dot(p_ctx.astype(BF16), vc_ref[:, ls])
        for d in range(3):
            o = o + _dot(p_loc[d].astype(BF16), vrefs[d][:, ls])
        o_ref[:, ls] = (o / l).astype(o_ref.dtype)


def _na_attn(q, k, v, kc, vc, bias):
    B, T, W = q.shape
    Lc = kc.shape[1]
    H = W // HEAD_DIM
    hps = 2 if H % 2 == 0 else 1
    hw = hps * HEAD_DIM
    nq = NA_QROWS * GRID_W
    nblk = T // nq
    assert T % nq == 0 and nblk >= 4
    grid = (H // hps, B, nblk)

    def kmap(d):
        return lambda h, b, i: (b, jnp.clip(i - 1, 0, nblk - 3) + d, h)

    qspec = pl.BlockSpec((None, nq, hw), lambda h, b, i: (b, i, h))
    kspecs = [pl.BlockSpec((None, nq, hw), kmap(d)) for d in range(3)]
    cspec = pl.BlockSpec((None, Lc, hw), lambda h, b, i: (b, 0, h))
    bspec = pl.BlockSpec(
        (None, hps, nq, 3 * nq),
        lambda h, b, i: (jnp.where(i == 0, 0, jnp.where(i == nblk - 1, 2, 1)), h, 0, 0))
    return pl.pallas_call(
        functools.partial(_na_kernel, hps), grid=grid,
        in_specs=[qspec] + kspecs + kspecs + [cspec, cspec, bspec], out_specs=qspec,
        out_shape=jax.ShapeDtypeStruct((B, T, W), BF16),
        compiler_params=_cparams(("parallel", "parallel", "arbitrary")), name="na_attn",
    )(q, k, k, k, v, v, v, kc, vc, bias)


def _dense_kernel(q_ref, k_ref, v_ref, o_ref):
    s = _dot_nt(q_ref[...], k_ref[...])
    m = jnp.max(s, axis=-1, keepdims=True)
    p = jnp.exp2(s - m)
    l = jnp.sum(p, axis=-1, keepdims=True)
    o_ref[...] = (_dot(p.astype(BF16), v_ref[...]) / l).astype(o_ref.dtype)


def _dense_attn(q, k, v):
    B, L, W = q.shape
    H = W // HEAD_DIM
    spec = pl.BlockSpec((None, L, HEAD_DIM), lambda b, h: (b, 0, h))
    return pl.pallas_call(
        _dense_kernel, grid=(B, H),
        in_specs=[spec, spec, spec], out_specs=spec,
        out_shape=jax.ShapeDtypeStruct((B, L, W), BF16),
        compiler_params=_cparams(("parallel", "parallel")), name="dense_attn",
    )(q, k, v)


def _diff_kernel(tk, n_lat, post_scale, lam_ref, q_ref, kc_ref, vc_ref, *refs):
    if n_lat:
        k_ref, v_ref, g_ref, o_ref, qs_sc, m_sc, acc_sc, vcx_sc, vx_sc, sa_sc, sb_sc = refs
    else:
        g_ref, o_ref, qs_sc, m_sc, acc_sc, vcx_sc = refs
    q = q_ref[...]
    tq = q.shape[0]

    def with_ones(dst, src):
        lane = lax.broadcasted_iota(jnp.int32, src.shape, 1)
        dst[:, 0:HEAD_DIM] = src[...]
        dst[:, HEAD_DIM:] = jnp.where(lane == 0, 1.0, 0.0).astype(BF16)

    @pl.when(pl.program_id(2) == 0)
    def _():
        with_ones(vcx_sc, vc_ref)
        if n_lat:
            with_ones(vx_sc, v_ref)

    lane = lax.broadcasted_iota(jnp.int32, q.shape, 1)
    zero = jnp.zeros_like(q)
    qs_sc[0:tq, :] = jnp.where(lane < DIFF_DIM, q, zero)
    qs_sc[tq:, :] = jnp.where(lane >= DIFF_DIM, q, zero)
    m_sc[...] = jnp.full(m_sc.shape, -jnp.inf, F32)
    acc_sc[...] = jnp.zeros(acc_sc.shape, F32)

    def softmax_pv(s, vb):
        m_old = m_sc[...]
        m_new = jnp.maximum(m_old, jnp.max(s, axis=-1, keepdims=True))
        alpha = jnp.exp2(m_old - m_new)
        p = jnp.exp2((s - _lane_tile(m_new, s.shape[1] // LANES)).astype(BF16))
        acc_sc[...] = _lane_tile(alpha, 2) * acc_sc[...] + _dot(p, vb)
        m_sc[...] = m_new

    s_ctx = _dot_nt(qs_sc[...], kc_ref[...])
    if n_lat:
        sa_sc[...] = _dot_nt(qs_sc[...], k_ref[pl.ds(0, tk), :])
    softmax_pv(s_ctx, vcx_sc[...])
    if n_lat:

        def body(c2, carry):
            c = 2 * c2
            off0 = pl.multiple_of(c * tk, tk)
            off1 = pl.multiple_of((c + 1) * tk, tk)
            off2 = pl.multiple_of(jnp.minimum(c + 2, n_lat - 1) * tk, tk)
            sb_sc[...] = _dot_nt(qs_sc[...], k_ref[pl.ds(off1, tk), :])
            softmax_pv(sa_sc[...], vx_sc[pl.ds(off0, tk), :])
            sa_sc[...] = _dot_nt(qs_sc[...], k_ref[pl.ds(off2, tk), :])
            softmax_pv(sb_sc[...], vx_sc[pl.ds(off1, tk), :])
            return carry

        lax.fori_loop(0, n_lat // 2, body, 0)
    acc = acc_sc[...]
    o = acc[:, :HEAD_DIM] / acc[:, HEAD_DIM:HEAD_DIM + 1]
    o = o[:tq] - lam_ref[0, 0] * o[tq:]
    o = o * lax.rsqrt(jnp.mean(o * o, axis=-1, keepdims=True) + EPS)
    o_ref[...] = (o * g_ref[...] * post_scale).astype(o_ref.dtype)


def _diff_attn(q, kc, vc, k, v, lam, gain, post_scale):
    B, L, W = q.shape
    Lc = kc.shape[1]
    H = W // HEAD_DIM
    tq = _pick(L, 512, 8)
    grid = (B, H, L // tq)
    qspec = pl.BlockSpec((None, tq, HEAD_DIM), lambda b, h, i: (b, i, h))
    cspec = pl.BlockSpec((None, Lc, HEAD_DIM), lambda b, h, i: (b, 0, h))
    ins = [lam.reshape(1, 1).astype(F32), q, kc, vc]
    specs = [pl.BlockSpec(memory_space=pltpu.SMEM), qspec, cspec, cspec]
    scratch = [pltpu.VMEM((2 * tq, HEAD_DIM), BF16), pltpu.VMEM((2 * tq, LANES), F32),
               pltpu.VMEM((2 * tq, 2 * HEAD_DIM), F32), pltpu.VMEM((Lc, 2 * HEAD_DIM), BF16)]
    tk, n_lat = 0, 0
    if k is not None:
        T = k.shape[1]
        tk = _pick(T, 1024, 8)
        n_lat = T // tk
        assert n_lat % 2 == 0
        lspec = pl.BlockSpec((None, T, HEAD_DIM), lambda b, h, i: (b, 0, h))
        ins += [k, v]
        specs += [lspec, lspec]
        scratch += [pltpu.VMEM((T, 2 * HEAD_DIM), BF16),
                    pltpu.VMEM((2 * tq, tk), F32), pltpu.VMEM((2 * tq, tk), F32)]
    ins.append(gain.reshape(1, HEAD_DIM).astype(F32))
    specs.append(pl.BlockSpec((1, HEAD_DIM), lambda b, h, i: (0, 0)))
    return pl.pallas_call(
        functools.partial(_diff_kernel, tk, n_lat, post_scale), grid=grid,
        in_specs=specs, out_specs=qspec,
        out_shape=jax.ShapeDtypeStruct((B, L, W), BF16), scratch_shapes=scratch,
        compiler_params=_cparams(("parallel", "parallel", "arbitrary")), name="diff_attn",
    )(*ins)


CONV_HALO = 16
CONV_RC = 32


def _conv_kernel(cw, nt, prev_ref, cur_ref, next_ref, dw_ref, db_ref, g_ref, b_ref,
                 pw_ref, o_ref, pad_ref, y_ref):
    i = pl.program_id(1)
    tt, ch = cur_ref.shape
    half = cw // 2
    npad = tt + 2 * CONV_HALO
    pad_ref[0, 0:CONV_HALO, :] = jnp.where(i == 0, 0.0, prev_ref[...])
    pad_ref[0, CONV_HALO:CONV_HALO + tt, :] = cur_ref[...]
    pad_ref[0, CONV_HALO + tt:, :] = jnp.where(i == nt - 1, 0.0, next_ref[...])
    for b in range(1, SUBLANES):
        pad_ref[b, 0:npad - SUBLANES, :] = pad_ref[0, b:b + npad - SUBLANES, :]
    base = CONV_HALO - half
    for r0 in range(0, tt, CONV_RC):
        for c0 in range(0, ch, LANES):
            acc = jnp.zeros((CONV_RC, LANES), F32)
            for j in range(cw):
                a, b = divmod(base + j, SUBLANES)
                acc = acc + dw_ref[j:j + 1, c0:c0 + LANES] * \
                    pad_ref[b, r0 + SUBLANES * a:r0 + SUBLANES * a + CONV_RC, c0:c0 + LANES]
            y_ref[r0:r0 + CONV_RC, c0:c0 + LANES] = acc
    y = y_ref[...] + db_ref[...]
    mu = jnp.mean(y, axis=-1, keepdims=True)
    yc = y - mu
    var = jnp.mean(yc * yc, axis=-1, keepdims=True)
    z = yc * lax.rsqrt(var + EPS) * g_ref[...] + b_ref[...]
    z = z * jax.nn.sigmoid(z)
    o_ref[...] = _dot(z.astype(BF16), pw_ref[...]).astype(o_ref.dtype)


def _conv(u, dw, db, ln_g, ln_b, pw):
    B, L, C = u.shape
    cw = dw.shape[0]
    assert cw // 2 <= CONV_HALO
    tt = _pick(L, 256, CONV_RC)
    nt = L // tt
    hb = tt // CONV_HALO
    nh = L // CONV_HALO
    cur = pl.BlockSpec((None, tt, C), lambda b, i: (b, i, 0))
    prev = pl.BlockSpec((None, CONV_HALO, C), lambda b, i: (b, jnp.maximum(i * hb - 1, 0), 0))
    nxt = pl.BlockSpec((None, CONV_HALO, C), lambda b, i: (b, jnp.minimum((i + 1) * hb, nh - 1), 0))
    vec = pl.BlockSpec((1, C), lambda b, i: (0, 0))
    return pl.pallas_call(
        functools.partial(_conv_kernel, cw, nt), grid=(B, nt),
        in_specs=[prev, cur, nxt, pl.BlockSpec((cw, C), lambda b, i: (0, 0)), vec, vec, vec,
                  pl.BlockSpec((C, C), lambda b, i: (0, 0))],
        out_specs=cur, out_shape=jax.ShapeDtypeStruct((B, L, C), BF16),
        scratch_shapes=[pltpu.VMEM((SUBLANES, tt + 2 * CONV_HALO, C), F32), pltpu.VMEM((tt, C), F32)],
        compiler_params=_cparams(("parallel", "parallel")), name="conv",
    )(u, u, u, dw.astype(F32), db.reshape(1, C).astype(F32), ln_g.reshape(1, C).astype(F32),
      ln_b.reshape(1, C).astype(F32), pw)


def _outproj_kernel(ka, kb, a_ref, b_ref, c_ref, w_ref, x_ref, g_ref, o_ref):
    acc = _dot(a_ref[...], w_ref[0:ka, :].astype(BF16))
    acc = acc + _dot(b_ref[...], w_ref[ka:ka + kb, :].astype(BF16))
    acc = acc + _dot(c_ref[...], w_ref[ka + kb:, :].astype(BF16))
    o_ref[...] = x_ref[...] + g_ref[...] * acc


def _outproj(a, b, c, w, x, gate):
    w, layer = w
    B, L, D = x.shape
    ka, kb, kc = a.shape[2], b.shape[2], c.shape[2]
    K = ka + kb + kc
    tm = _pick(L, 1024, 8)
    tn = _pick(D, 512)
    grid = (B, L // tm, D // tn)

    def aspec(k):
        return pl.BlockSpec((None, tm, k), lambda bb, i, j: (bb, i, 0))

    xspec = pl.BlockSpec((None, tm, tn), lambda bb, i, j: (bb, i, j))
    return pl.pallas_call(
        functools.partial(_outproj_kernel, ka, kb), grid=grid,
        in_specs=[aspec(ka), aspec(kb), aspec(kc),
                  pl.BlockSpec((None, K, tn), lambda bb, i, j: (layer, 0, j)), xspec,
                  pl.BlockSpec((None, 1, tn), lambda bb, i, j: (bb, 0, j))],
        out_specs=xspec, out_shape=jax.ShapeDtypeStruct((B, L, D), F32),
        compiler_params=_cparams(("parallel", "parallel", "arbitrary")), name="outproj",
    )(a, b, c, w, x, gate)


def _ffn_kernel(xs_ref, wg_ref, wu_ref, wd_ref, g_ref, o_ref):
    xs = xs_ref[...]
    a = _dot(xs, wg_ref[...])
    u = _dot(xs, wu_ref[...])
    hmid = (a * jax.nn.sigmoid(a) * u).astype(BF16)
    o_ref[...] = (_dot(hmid, wd_ref[...]) * g_ref[...]).astype(o_ref.dtype)


def _experts(xs, wg, wu, wd, g):
    B, E, C, D = xs.shape
    Fd = wg.shape[2]
    tc = _pick(C, 512, 8)
    grid = (E, B, C // tc)
    return pl.pallas_call(
        _ffn_kernel, grid=grid,
        in_specs=[pl.BlockSpec((None, None, tc, D), lambda e, b, i: (b, e, i, 0)),
                  pl.BlockSpec((None, D, Fd), lambda e, b, i: (e, 0, 0)),
                  pl.BlockSpec((None, D, Fd), lambda e, b, i: (e, 0, 0)),
                  pl.BlockSpec((None, Fd, D), lambda e, b, i: (e, 0, 0)),
                  pl.BlockSpec((None, None, tc, 1), lambda e, b, i: (b, e, i, 0))],
        out_specs=pl.BlockSpec((None, None, tc, D), lambda e, b, i: (b, e, i, 0)),
        out_shape=jax.ShapeDtypeStruct((B, E, C, D), BF16),
        compiler_params=_cparams(("parallel", "parallel", "arbitrary")), name="experts",
    )(xs, wg, wu, wd, g)


MOE_TT = 256
MOE_RB = 256


def _combine_kernel(tt, tile_ref, blk_ref, flag_ref, tok_ref, y_ref, x_ref, g_ref, o_ref, acc_sc):
    b, j = pl.program_id(0), pl.program_id(1)
    fl = flag_ref[b, j]

    @pl.when((fl & 1) != 0)
    def _():
        acc_sc[...] = jnp.zeros(acc_sc.shape, F32)

    @pl.when((fl & 4) != 0)
    def _():
        rb = y_ref.shape[0]
        toks = tile_ref[b, j] * tt + lax.broadcasted_iota(jnp.int32, (tt, rb), 0)
        sel = jnp.where(tok_ref[...] == toks, 1.0, 0.0).astype(BF16)
        acc_sc[...] += _dot(sel, y_ref[...])

    @pl.when((fl & 2) != 0)
    def _():
        o_ref[...] = x_ref[...] + g_ref[...] * acc_sc[...]


def _combine_plan(tok_sorted, L, tt, rb):
    B, R = tok_sorted.shape
    ntile, nblk = L // tt, R // rb
    bounds = jnp.arange(ntile + 1, dtype=jnp.int32) * tt
    cut = jax.vmap(lambda ts: jnp.searchsorted(ts, bounds, side="left"))(tok_sorted).astype(jnp.int32)
    lo, hi = cut[:, :-1], cut[:, 1:]
    kfirst = jnp.minimum(lo // rb, nblk - 1)
    klast = jnp.maximum(kfirst, (hi - 1) // rb)
    nb = klast - kfirst + 1
    ends = jnp.cumsum(nb, axis=1)
    starts = ends - nb
    ns = ntile + nblk
    j = jnp.arange(ns, dtype=jnp.int32)
    tile = jax.vmap(lambda st: jnp.searchsorted(st, j, side="right"))(starts).astype(jnp.int32) - 1
    tile = jnp.clip(tile, 0, ntile - 1)
    off = j[None, :] - jnp.take_along_axis(starts, tile, axis=1)
    nb_t = jnp.take_along_axis(nb, tile, axis=1)
    valid = j[None, :] < ends[:, -1:]
    blk = jnp.clip(jnp.take_along_axis(kfirst, tile, axis=1) + off, 0, nblk - 1)
    flags = (valid & (off == 0)) * 1 + (valid & (off == nb_t - 1)) * 2 + valid * 4
    return tile, blk.astype(jnp.int32), flags.astype(jnp.int32)


def _combine(y_sorted, tok_sorted, xres, gate):
    B, L, D = xres.shape
    R = y_sorted.shape[1]
    tt = _pick(L, MOE_TT, 8)
    rb = _pick(R, MOE_RB)
    tile, blk, flags = _combine_plan(tok_sorted, L, tt, rb)
    ns = tile.shape[1]
    tok4 = tok_sorted.reshape(B, R // rb, 1, rb)
    grid_spec = pltpu.PrefetchScalarGridSpec(
        num_scalar_prefetch=3, grid=(B, ns),
        in_specs=[
            pl.BlockSpec((None, None, 1, rb), lambda b, j, t, k, f: (b, k[b, j], 0, 0)),
            pl.BlockSpec((None, rb, D), lambda b, j, t, k, f: (b, k[b, j], 0)),
            pl.BlockSpec((None, tt, D), lambda b, j, t, k, f: (b, t[b, j], 0)),
            pl.BlockSpec((None, 1, D), lambda b, j, t, k, f: (b, 0, 0)),
        ],
        out_specs=pl.BlockSpec((None, tt, D), lambda b, j, t, k, f: (b, t[b, j], 0)),
        scratch_shapes=[pltpu.VMEM((tt, D), F32)])
    return pl.pallas_call(
        functools.partial(_combine_kernel, tt), grid_spec=grid_spec,
        out_shape=jax.ShapeDtypeStruct((B, L, D), F32),
        compiler_params=_cparams(("parallel", "arbitrary")), name="moe_combine",
    )(tile, blk, flags, tok4, y_sorted, xres, gate)


def _ec_moe(xres, gate, scale, shift, w_router, wg, wu, wd):
    B, L, D = xres.shape
    E = w_router.shape[1]
    cap = max(1, EC_CAPACITY * L // E)
    h2, logits = _norm_mod(xres, scale, shift, w_router)
    aff = jax.nn.softmax(logits[..., :E], axis=-1)
    g, idx = lax.top_k(jnp.swapaxes(aff, 1, 2), cap)
    bidx = jnp.arange(B)[:, None, None]
    xs = h2[bidx, idx]
    y = _experts(xs, wg, wu, wd, g[..., None]).reshape(B, E * cap, D)
    flat_tok = idx.reshape(B, E * cap).astype(jnp.int32)
    order = jnp.argsort(flat_tok, axis=1)
    tok_sorted = jnp.take_along_axis(flat_tok, order, axis=1)
    y_sorted = jnp.take_along_axis(y, order[:, :, None], axis=1)
    return _combine(y_sorted, tok_sorted, xres, gate)


def _ada(cvec, down, up, bias, n, D):
    hi = lax.Precision.HIGHEST
    z = jnp.dot(jax.nn.silu(cvec), down, precision=hi)
    m = jnp.dot(z, up[:, :n * D], precision=hi) + bias[:n * D]
    return m.reshape(cvec.shape[:-1] + (n, D))


def _rope_tables(T):
    nf = DIFF_DIM // 4
    inv = ROPE_BASE ** (-jnp.arange(nf, dtype=F32) / nf)
    t_idx = jnp.arange(T)
    ang_r = (t_idx // GRID_W).astype(F32)[:, None] * inv
    ang_c = (t_idx % GRID_W).astype(F32)[:, None] * inv
    cr, sr, cc, sc = jnp.cos(ang_r), jnp.sin(ang_r), jnp.cos(ang_c), jnp.sin(ang_c)
    cos = jnp.concatenate([cr, cr, cc, cc] * 2, axis=-1)
    sin = jnp.concatenate([-sr, sr, -sc, sc] * 2, axis=-1)
    return cos, sin


def kernel(x, c, ctx, c_ctx, ada_down, ada_up, ada_bias, w_in, conv_dw, conv_db, conv_ln_g,
           conv_ln_b, conv_pw, na_q_gain, na_k_gain, na_rpb, diff_q_gain, diff_k_gain, diff_lam,
           diff_out_gain, w_out, w_router, w_gate, w_up, w_down):
    B, T, D = x.shape
    Lc = ctx.shape[1]
    depth = w_in.shape[0]
    na_w = 3 * D // 8
    df_w = 3 * D // 8
    cch = D // 4
    o_kna, o_vna, o_kdf, o_vdf = 0, na_w, 2 * na_w, 2 * na_w + df_w
    o_qna = 2 * na_w + 2 * df_w
    o_qdf = o_qna + na_w
    o_cv = o_qdf + df_w
    o_cg = o_cv + cch
    cos, sin = _rope_tables(T)
    na_bias = _na_bias(na_rpb.astype(F32) * LOG2E)

    for l in range(depth):
        update_ctx = l < depth - 1
        lam_init = 0.8 - 0.6 * math.exp(-0.3 * l)
        lv = diff_lam[l].astype(F32)
        lam = jnp.exp(jnp.sum(lv[0] * lv[1])) - jnp.exp(jnp.sum(lv[2] * lv[3])) + lam_init
        post = 1.0 - lam_init
        m = _ada(c, ada_down[l], ada_up[l], ada_bias[l], N_MOD, D)
        mc = _ada(c_ctx, ada_down[l], ada_up[l], ada_bias[l], N_MOD if update_ctx else 2, D)
        mcb = jnp.broadcast_to(mc[None], (B,) + mc.shape)
        wl = (w_in, l)
        wo = (w_out, l)
        pw = conv_pw[l].astype(BF16)
        wg, wu, wd = w_gate[l].astype(BF16), w_up[l].astype(BF16), w_down[l].astype(BF16)
        qg_df = diff_q_gain[l].reshape(-1) * (DIFF_DIM ** -0.5 * LOG2E)
        qg_na = na_q_gain[l] * (HEAD_DIM ** -0.5 * LOG2E)
        kg_df = diff_k_gain[l].reshape(-1)

        hc = _norm_mod(ctx, mcb[:, 1:2], mcb[:, 0:1]).reshape(B * Lc, D)
        kna_c = _proj(hc, wl, o_kna, na_w, "rms128", gain=na_k_gain[l]).reshape(B, Lc, na_w)
        vna_c = _proj(hc, wl, o_vna, na_w, "plain").reshape(B, Lc, na_w)
        kdf_c = _proj(hc, wl, o_kdf, df_w, "rms64", gain=kg_df).reshape(B, Lc, df_w)
        vdf_c = _proj(hc, wl, o_vdf, df_w, "plain").reshape(B, Lc, df_w)

        h = _norm_mod(x, m[:, 1:2], m[:, 0:1]).reshape(B * T, D)
        k_na = _proj(h, wl, o_kna, na_w, "rms128", gain=na_k_gain[l]).reshape(B, T, na_w)
        v_na = _proj(h, wl, o_vna, na_w, "plain").reshape(B, T, na_w)
        k_df = _proj(h, wl, o_kdf, df_w, "rms64", gain=kg_df, cos=cos, sin=sin).reshape(B, T, df_w)
        v_df = _proj(h, wl, o_vdf, df_w, "plain").reshape(B, T, df_w)
        q_na = _proj(h, wl, o_qna, na_w, "rms128", gain=qg_na).reshape(B, T, na_w)
        q_df = _proj(h, wl, o_qdf, df_w, "rms64", gain=qg_df, cos=cos, sin=sin).reshape(B, T, df_w)
        u = _proj(h, wl, o_cv, cch, "glu", col0_b=o_cg).reshape(B, T, cch)

        o_na = _na_attn(q_na, k_na, v_na, kna_c, vna_c, na_bias[l])
        o_df = _diff_attn(q_df, kdf_c, vdf_c, k_df, v_df, lam, diff_out_gain[l], post)
        o_cv_ = _conv(u, conv_dw[l], conv_db[l], conv_ln_g[l], conv_ln_b[l], pw)
        x = _outproj(o_cv_, o_na, o_df, wo, x, m[:, 2:3])
        x = _ec_moe(x, m[:, 5:6], m[:, 4:5], m[:, 3:4], w_router[l], wg, wu, wd)

        if update_ctx:
            qna_c = _proj(hc, wl, o_qna, na_w, "rms128", gain=qg_na).reshape(B, Lc, na_w)
            qdf_c = _proj(hc, wl, o_qdf, df_w, "rms64", gain=qg_df).reshape(B, Lc, df_w)
            u_c = _proj(hc, wl, o_cv, cch, "glu", col0_b=o_cg).reshape(B, Lc, cch)
            o_na_c = _dense_attn(qna_c, kna_c, vna_c)
            o_df_c = _diff_attn(qdf_c, kdf_c, vdf_c, None, None, lam, diff_out_gain[l], post)
            o_cv_c = _conv(u_c, conv_dw[l], conv_db[l], conv_ln_g[l], conv_ln_b[l], pw)
            ctx = _outproj(o_cv_c, o_na_c, o_df_c, wo, ctx, mcb[:, 2:3])
            ctx = _ec_moe(ctx, mcb[:, 5:6], mcb[:, 4:5], mcb[:, 3:4], w_router[l], wg, wu, wd)
    return x
```

```python
import functools
import math

import numpy as np
import jax
import jax.numpy as jnp
from jax import lax
from jax.experimental import pallas as pl
from jax.experimental.pallas import tpu as pltpu

GRID_W = 64
HEAD_DIM = 128
WIN_R = 8
WIN_C = 16
DIFF_DIM = HEAD_DIM // 2
ROPE_BASE = 10000.0
EC_CAPACITY = 2
N_MOD = 6
EPS = 1e-6
LOG2E = math.log2(math.e)
LANES = 128
SUBLANES = 8
NA_QROWS = 4
NA_HPS = 6
NEG = -1e30
VMEM_LIMIT = 56 * 1024 * 1024

F32 = jnp.float32
BF16 = jnp.bfloat16


def _cparams(sem):
    return pltpu.CompilerParams(dimension_semantics=sem, vmem_limit_bytes=VMEM_LIMIT)


def _pick(n, cap, mult=LANES):
    best = None
    for t in range(mult, min(n, cap) + 1, mult):
        if n % t == 0:
            best = t
    assert best is not None, (n, cap, mult)
    return best


def _dot(a, b):
    return jnp.dot(a, b, preferred_element_type=F32)


def _dot_nt(a, b):
    return lax.dot_general(a, b, (((1,), (1,)), ((), ())), preferred_element_type=F32)


def _lane_fold(op, arrays):
    tiles = [a[:, t * LANES:(t + 1) * LANES] for a in arrays for t in range(a.shape[1] // LANES)]
    return functools.reduce(op, tiles)


def _with_ones(v):
    lane = lax.broadcasted_iota(jnp.int32, v.shape, 1)
    return jnp.concatenate([v, jnp.where(lane == 0, 1.0, 0.0).astype(v.dtype)], axis=1)


def _lane_tile(x, k):
    return jnp.concatenate([x] * k, axis=1)


def _norm_kernel(x_ref, sc_ref, sh_ref, o_ref):
    x = x_ref[...]
    ms = jnp.mean(x * x, axis=-1, keepdims=True)
    h = x * lax.rsqrt(ms + EPS) * (1.0 + sc_ref[...]) + sh_ref[...]
    o_ref[...] = h.astype(o_ref.dtype)


def _norm_router_kernel(x_ref, sc_ref, sh_ref, wr_ref, o_ref, lg_ref):
    x = x_ref[...]
    ms = jnp.mean(x * x, axis=-1, keepdims=True)
    h = (x * lax.rsqrt(ms + EPS) * (1.0 + sc_ref[...]) + sh_ref[...]).astype(o_ref.dtype)
    o_ref[...] = h
    lg_ref[...] = _dot(h, wr_ref[...])


def _norm_mod(x, scale, shift, w_router=None):
    B, L, D = x.shape
    tl = _pick(L, 512, 8)
    grid = (B, L // tl)
    xspec = pl.BlockSpec((None, tl, D), lambda b, i: (b, i, 0))
    mspec = pl.BlockSpec((None, 1, D), lambda b, i: (b, 0, 0))
    if w_router is None:
        return pl.pallas_call(
            _norm_kernel, grid=grid, in_specs=[xspec, mspec, mspec], out_specs=xspec,
            out_shape=jax.ShapeDtypeStruct((B, L, D), BF16),
            compiler_params=_cparams(("parallel", "parallel")), name="norm_mod",
        )(x, scale, shift)
    E = w_router.shape[1]
    wr = jnp.zeros((D, LANES), BF16).at[:, :E].set(w_router.astype(BF16))
    return pl.pallas_call(
        _norm_router_kernel, grid=grid,
        in_specs=[xspec, mspec, mspec, pl.BlockSpec((D, LANES), lambda b, i: (0, 0))],
        out_specs=[xspec, pl.BlockSpec((None, tl, LANES), lambda b, i: (b, i, 0))],
        out_shape=[jax.ShapeDtypeStruct((B, L, D), BF16),
                   jax.ShapeDtypeStruct((B, L, LANES), F32)],
        compiler_params=_cparams(("parallel", "parallel")), name="norm_router",
    )(x, scale, shift, wr)


PROJ_RCH = 256


def _proj_kernel(kind, rope, tn, *refs):
    if kind == "glu":
        h_ref, w_ref, w2_ref, o_ref = refs
        h = h_ref[...]
        val = _dot(h, w_ref[...])
        gate = _dot(h, w2_ref[...])
        o_ref[...] = val * jax.nn.sigmoid(gate)
        return
    if kind == "plain":
        h_ref, w_ref, o_ref = refs
        o_ref[...] = _dot(h_ref[...], w_ref[...]).astype(o_ref.dtype)
        return
    if rope:
        h_ref, w_ref, g_ref, cos_ref, sin_ref, o_ref = refs
    else:
        h_ref, w_ref, g_ref, o_ref = refs
    tm = h_ref.shape[0]
    rch = min(tm, PROJ_RCH)
    for r0 in range(0, tm, rch):
        acc = _dot(h_ref[r0:r0 + rch, :], w_ref[...])
        for s in range(tn // LANES):
            t = acc[:, s * LANES:(s + 1) * LANES]
            sq = t * t
            lane = lax.broadcasted_iota(jnp.int32, t.shape, 1)
            if kind == "rms128":
                inv = lax.rsqrt(jnp.mean(sq, axis=-1, keepdims=True) + EPS)
            else:
                lo = jnp.sum(jnp.where(lane < DIFF_DIM, sq, 0.0), axis=-1, keepdims=True)
                tot = jnp.sum(sq, axis=-1, keepdims=True)
                inv = jnp.where(lane < DIFF_DIM,
                                lax.rsqrt(lo * (1.0 / DIFF_DIM) + EPS),
                                lax.rsqrt((tot - lo) * (1.0 / DIFF_DIM) + EPS))
            t = t * inv * g_ref[...]
            if rope:
                up = pltpu.roll(t, LANES - 16, axis=1)
                dn = pltpu.roll(t, 16, axis=1)
                sw = jnp.where((lane % 32) < 16, up, dn)
                t = t * cos_ref[r0:r0 + rch, :] + sw * sin_ref[r0:r0 + rch, :]
            o_ref[r0:r0 + rch, s * LANES:(s + 1) * LANES] = t.astype(o_ref.dtype)


def _proj(h, w, col0, ncols, kind, *, gain=None, cos=None, sin=None, col0_b=None):
    w, layer = w
    M, D = h.shape
    tm = _pick(M, 1024, 8)
    tn = _pick(ncols, 512)
    assert col0 % tn == 0
    grid = (M // tm, ncols // tn)
    hspec = pl.BlockSpec((tm, D), lambda i, j: (i, 0))
    c0 = col0 // tn
    wspec = pl.BlockSpec((None, D, tn), lambda i, j: (layer, 0, c0 + j))
    ospec = pl.BlockSpec((tm, tn), lambda i, j: (i, j))
    rope = cos is not None
    ins, specs = [h, w], [hspec, wspec]
    odt = BF16
    if kind == "glu":
        assert col0_b % tn == 0
        c1 = col0_b // tn
        ins.append(w)
        specs.append(pl.BlockSpec((None, D, tn), lambda i, j: (layer, 0, c1 + j)))
        odt = F32
    elif kind != "plain":
        ins.append(gain.reshape(1, LANES).astype(F32))
        specs.append(pl.BlockSpec((1, LANES), lambda i, j: (0, 0)))
        if rope:
            nt = cos.shape[0] // tm
            tspec = pl.BlockSpec((tm, LANES), lambda i, j: (i % nt, 0))
            ins += [cos, sin]
            specs += [tspec, tspec]
    return pl.pallas_call(
        functools.partial(_proj_kernel, kind, rope, tn), grid=grid,
        in_specs=specs, out_specs=ospec,
        out_shape=jax.ShapeDtypeStruct((M, ncols), odt),
        compiler_params=_cparams(("parallel", "arbitrary")), name="proj_" + kind,
    )(*ins)


def _na_bias(rpb):
    L, H = rpb.shape[:2]
    qc = np.arange(GRID_W)
    kc = np.arange(GRID_W)
    cs = np.clip(qc - WIN_C // 2, 0, GRID_W - WIN_C)
    col_ok = (kc[None, :] >= cs[:, None]) & (kc[None, :] < cs[:, None] + WIN_C)
    dc = kc[None, :] - qc[:, None] + WIN_C - 1
    onehot = (dc[None] == np.arange(2 * WIN_C - 1)[:, None, None]) & col_ok[None]
    tq = jnp.einsum("lhrc,cqk->lhqrk", rpb.astype(F32), jnp.asarray(onehot, F32),
                    precision=lax.Precision.HIGHEST)
    tq = jnp.where(jnp.asarray(col_ok)[:, None, :], tq, NEG)
    nkr = 3 * NA_QROWS

    def neg(n):
        return jnp.full((L, H, GRID_W, n, GRID_W), NEG, F32)

    tabs = []
    for typ in range(3):
        rows_q = []
        for qr in range(NA_QROWS):
            if typ == 0:
                k0, d0 = 0, WIN_R - 1 - qr
            elif typ == 1:
                k0, d0 = qr, WIN_R // 2 - 1
            else:
                k0, d0 = nkr - WIN_R, nkr - 2 * NA_QROWS - 1 - qr
            parts = [tq[:, :, :, d0:d0 + WIN_R, :]]
            if k0:
                parts.insert(0, neg(k0))
            if nkr - k0 - WIN_R:
                parts.append(neg(nkr - k0 - WIN_R))
            rows_q.append(jnp.concatenate(parts, axis=3).reshape(L, H, GRID_W, nkr * GRID_W))
        tabs.append(jnp.concatenate(rows_q, axis=2))
    return jnp.stack(tabs, axis=1)


def _na_kernel(hps, q_ref, k0_ref, k1_ref, k2_ref, v0_ref, v1_ref, v2_ref,
               kc_ref, vc_ref, b_ref, o_ref):
    nq = q_ref.shape[0]
    krefs = (k0_ref, k1_ref, k2_ref)
    vrefs = (v0_ref, v1_ref, v2_ref)
    for hh in range(hps):
        ls = slice(hh * HEAD_DIM, (hh + 1) * HEAD_DIM)
        q = q_ref[:, ls]
        s_loc = [_dot_nt(q, krefs[d][:, ls]) + b_ref[hh, :, d * nq:(d + 1) * nq] for d in range(3)]
        s_ctx = _dot_nt(q, kc_ref[:, ls])
        m = jnp.max(_lane_fold(jnp.maximum, s_loc + [s_ctx]), axis=-1, keepdims=True)
        o = _dot(jnp.exp2((s_ctx - m).astype(BF16)), _with_ones(vc_ref[:, ls]))
        for d in range(3):
            o = o + _dot(jnp.exp2((s_loc[d] - m).astype(BF16)), _with_ones(vrefs[d][:, ls]))
        o_ref[:, ls] = (o[:, :HEAD_DIM] / o[:, HEAD_DIM:HEAD_DIM + 1]).astype(o_ref.dtype)


def _na_attn(q, k, v, kc, vc, bias):
    B, T, W = q.shape
    Lc = kc.shape[1]
    H = W // HEAD_DIM
    hps = max(d for d in range(1, NA_HPS + 1) if H % d == 0)
    hw = hps * HEAD_DIM
    nq = NA_QROWS * GRID_W
    nblk = T // nq
    assert T % nq == 0 and nblk >= 4
    grid = (H // hps, B, nblk)

    def kmap(d):
        return lambda h, b, i: (b, jnp.clip(i - 1, 0, nblk - 3) + d, h)

    qspec = pl.BlockSpec((None, nq, hw), lambda h, b, i: (b, i, h))
    kspecs = [pl.BlockSpec((None, nq, hw), kmap(d)) for d in range(3)]
    cspec = pl.BlockSpec((None, Lc, hw), lambda h, b, i: (b, 0, h))
    bspec = pl.BlockSpec(
        (None, hps, nq, 3 * nq),
        lambda h, b, i: (jnp.where(i == 0, 0, jnp.where(i == nblk - 1, 2, 1)), h, 0, 0))
    return pl.pallas_call(
        functools.partial(_na_kernel, hps), grid=grid,
        in_specs=[qspec] + kspecs + kspecs + [cspec, cspec, bspec], out_specs=qspec,
        out_shape=jax.ShapeDtypeStruct((B, T, W), BF16),
        compiler_params=_cparams(("parallel", "parallel", "arbitrary")), name="na_attn",
    )(q, k, k, k, v, v, v, kc, vc, bias)


def _dense_kernel(q_ref, k_ref, v_ref, o_ref):
    s = _dot_nt(q_ref[...], k_ref[...])
    m = jnp.max(s, axis=-1, keepdims=True)
    p = jnp.exp2(s - m)
    l = jnp.sum(p, axis=-1, keepdims=True)
    o_ref[...] = (_dot(p.astype(BF16), v_ref[...]) / l).astype(o_ref.dtype)


def _dense_attn(q, k, v):
    B, L, W = q.shape
    H = W // HEAD_DIM
    spec = pl.BlockSpec((None, L, HEAD_DIM), lambda b, h: (b, 0, h))
    return pl.pallas_call(
        _dense_kernel, grid=(B, H),
        in_specs=[spec, spec, spec], out_specs=spec,
        out_shape=jax.ShapeDtypeStruct((B, L, W), BF16),
        compiler_params=_cparams(("parallel", "parallel")), name="dense_attn",
    )(q, k, v)


def _diff_kernel(tk, n_lat, post_scale, lam_ref, q_ref, kc_ref, vc_ref, *refs):
    if n_lat:
        k_ref, v_ref, g_ref, o_ref, qs_sc, m_sc, acc_sc, vcx_sc, vx_sc, sa_sc, sb_sc = refs
    else:
        g_ref, o_ref, qs_sc, m_sc, acc_sc, vcx_sc = refs
    q = q_ref[...]
    tq = q.shape[0]

    def with_ones(dst, src):
        lane = lax.broadcasted_iota(jnp.int32, src.shape, 1)
        dst[:, 0:HEAD_DIM] = src[...]
        dst[:, HEAD_DIM:] = jnp.where(lane == 0, 1.0, 0.0).astype(BF16)

    @pl.when(pl.program_id(2) == 0)
    def _():
        with_ones(vcx_sc, vc_ref)
        if n_lat:
            with_ones(vx_sc, v_ref)

    lane = lax.broadcasted_iota(jnp.int32, q.shape, 1)
    zero = jnp.zeros_like(q)
    qs_sc[0:tq, :] = jnp.where(lane < DIFF_DIM, q, zero)
    qs_sc[tq:, :] = jnp.where(lane >= DIFF_DIM, q, zero)
    m_sc[...] = jnp.full(m_sc.shape, -jnp.inf, F32)
    acc_sc[...] = jnp.zeros(acc_sc.shape, F32)

    def softmax_pv(s, vb):
        m_old = m_sc[...]
        m_new = jnp.maximum(m_old, jnp.max(s, axis=-1, keepdims=True))
        alpha = jnp.exp2(m_old - m_new)
        p = jnp.exp2((s - _lane_tile(m_new, s.shape[1] // LANES)).astype(BF16))
        acc_sc[...] = _lane_tile(alpha, 2) * acc_sc[...] + _dot(p, vb)
        m_sc[...] = m_new

    s_ctx = _dot_nt(qs_sc[...], kc_ref[...])
    if n_lat:
        sa_sc[...] = _dot_nt(qs_sc[...], k_ref[pl.ds(0, tk), :])
    softmax_pv(s_ctx, vcx_sc[...])
    if n_lat:

        def body(c2, carry):
            c = 2 * c2
            off0 = pl.multiple_of(c * tk, tk)
            off1 = pl.multiple_of((c + 1) * tk, tk)
            off2 = pl.multiple_of(jnp.minimum(c + 2, n_lat - 1) * tk, tk)
            sb_sc[...] = _dot_nt(qs_sc[...], k_ref[pl.ds(off1, tk), :])
            softmax_pv(sa_sc[...], vx_sc[pl.ds(off0, tk), :])
            sa_sc[...] = _dot_nt(qs_sc[...], k_ref[pl.ds(off2, tk), :])
            softmax_pv(sb_sc[...], vx_sc[pl.ds(off1, tk), :])
            return carry

        lax.fori_loop(0, n_lat // 2, body, 0)
    acc = acc_sc[...]
    o = acc[:, :HEAD_DIM] / acc[:, HEAD_DIM:HEAD_DIM + 1]
    o = o[:tq] - lam_ref[0, 0] * o[tq:]
    o = o * lax.rsqrt(jnp.mean(o * o, axis=-1, keepdims=True) + EPS)
    o_ref[...] = (o * g_ref[...] * post_scale).astype(o_ref.dtype)


def _diff_attn(q, kc, vc, k, v, lam, gain, post_scale):
    B, L, W = q.shape
    Lc = kc.shape[1]
    H = W // HEAD_DIM
    tq = _pick(L, 1024, 8)
    grid = (B, H, L // tq)
    qspec = pl.BlockSpec((None, tq, HEAD_DIM), lambda b, h, i: (b, i, h))
    cspec = pl.BlockSpec((None, Lc, HEAD_DIM), lambda b, h, i: (b, 0, h))
    ins = [lam.reshape(1, 1).astype(F32), q, kc, vc]
    specs = [pl.BlockSpec(memory_space=pltpu.SMEM), qspec, cspec, cspec]
    scratch = [pltpu.VMEM((2 * tq, HEAD_DIM), BF16), pltpu.VMEM((2 * tq, LANES), F32),
               pltpu.VMEM((2 * tq, 2 * HEAD_DIM), F32), pltpu.VMEM((Lc, 2 * HEAD_DIM), BF16)]
    tk, n_lat = 0, 0
    if k is not None:
        T = k.shape[1]
        tk = _pick(T, 1024, 8)
        n_lat = T // tk
        assert n_lat % 2 == 0
        lspec = pl.BlockSpec((None, T, HEAD_DIM), lambda b, h, i: (b, 0, h))
        ins += [k, v]
        specs += [lspec, lspec]
        scratch += [pltpu.VMEM((T, 2 * HEAD_DIM), BF16),
                    pltpu.VMEM((2 * tq, tk), F32), pltpu.VMEM((2 * tq, tk), F32)]
    ins.append(gain.reshape(1, HEAD_DIM).astype(F32))
    specs.append(pl.BlockSpec((1, HEAD_DIM), lambda b, h, i: (0, 0)))
    return pl.pallas_call(
        functools.partial(_diff_kernel, tk, n_lat, post_scale), grid=grid,
        in_specs=specs, out_specs=qspec,
        out_shape=jax.ShapeDtypeStruct((B, L, W), BF16), scratch_shapes=scratch,
        compiler_params=_cparams(("parallel", "parallel", "arbitrary")), name="diff_attn",
    )(*ins)


CONV_HALO = 16
CONV_RC = 32


def _conv_kernel(cw, nt, prev_ref, cur_ref, next_ref, dw_ref, db_ref, g_ref, b_ref,
                 pw_ref, o_ref, pad_ref, y_ref):
    i = pl.program_id(1)
    tt, ch = cur_ref.shape
    half = cw // 2
    npad = tt + 2 * CONV_HALO
    pad_ref[0, 0:CONV_HALO, :] = jnp.where(i == 0, 0.0, prev_ref[...])
    pad_ref[0, CONV_HALO:CONV_HALO + tt, :] = cur_ref[...]
    pad_ref[0, CONV_HALO + tt:, :] = jnp.where(i == nt - 1, 0.0, next_ref[...])
    for b in range(1, SUBLANES):
        pad_ref[b, 0:npad - SUBLANES, :] = pad_ref[0, b:b + npad - SUBLANES, :]
    base = CONV_HALO - half
    for r0 in range(0, tt, CONV_RC):
        for c0 in range(0, ch, LANES):
            acc = jnp.zeros((CONV_RC, LANES), F32)
            for j in range(cw):
                a, b = divmod(base + j, SUBLANES)
                acc = acc + dw_ref[j:j + 1, c0:c0 + LANES] * \
                    pad_ref[b, r0 + SUBLANES * a:r0 + SUBLANES * a + CONV_RC, c0:c0 + LANES]
            y_ref[r0:r0 + CONV_RC, c0:c0 + LANES] = acc
    y = y_ref[...] + db_ref[...]
    mu = jnp.mean(y, axis=-1, keepdims=True)
    yc = y - mu
    var = jnp.mean(yc * yc, axis=-1, keepdims=True)
    z = yc * lax.rsqrt(var + EPS) * g_ref[...] + b_ref[...]
    z = z * jax.nn.sigmoid(z)
    o_ref[...] = _dot(z.astype(BF16), pw_ref[...]).astype(o_ref.dtype)


def _conv(u, dw, db, ln_g, ln_b, pw):
    B, L, C = u.shape
    cw = dw.shape[0]
    assert cw // 2 <= CONV_HALO
    tt = _pick(L, 256, CONV_RC)
    nt = L // tt
    hb = tt // CONV_HALO
    nh = L // CONV_HALO
    cur = pl.BlockSpec((None, tt, C), lambda b, i: (b, i, 0))
    prev = pl.BlockSpec((None, CONV_HALO, C), lambda b, i: (b, jnp.maximum(i * hb - 1, 0), 0))
    nxt = pl.BlockSpec((None, CONV_HALO, C), lambda b, i: (b, jnp.minimum((i + 1) * hb, nh - 1), 0))
    vec = pl.BlockSpec((1, C), lambda b, i: (0, 0))
    return pl.pallas_call(
        functools.partial(_conv_kernel, cw, nt), grid=(B, nt),
        in_specs=[prev, cur, nxt, pl.BlockSpec((cw, C), lambda b, i: (0, 0)), vec, vec, vec,
                  pl.BlockSpec((C, C), lambda b, i: (0, 0))],
        out_specs=cur, out_shape=jax.ShapeDtypeStruct((B, L, C), BF16),
        scratch_shapes=[pltpu.VMEM((SUBLANES, tt + 2 * CONV_HALO, C), F32), pltpu.VMEM((tt, C), F32)],
        compiler_params=_cparams(("parallel", "parallel")), name="conv",
    )(u, u, u, dw.astype(F32), db.reshape(1, C).astype(F32), ln_g.reshape(1, C).astype(F32),
      ln_b.reshape(1, C).astype(F32), pw)


def _outproj_kernel(ka, kb, a_ref, b_ref, c_ref, w_ref, x_ref, g_ref, o_ref):
    acc = _dot(a_ref[...], w_ref[0:ka, :])
    acc = acc + _dot(b_ref[...], w_ref[ka:ka + kb, :])
    acc = acc + _dot(c_ref[...], w_ref[ka + kb:, :])
    o_ref[...] = x_ref[...] + g_ref[...] * acc


def _outproj(a, b, c, w, x, gate):
    w, layer = w
    B, L, D = x.shape
    ka, kb, kc = a.shape[2], b.shape[2], c.shape[2]
    K = ka + kb + kc
    tm = _pick(L, 1024, 8)
    tn = _pick(D, 512)
    grid = (B, L // tm, D // tn)

    def aspec(k):
        return pl.BlockSpec((None, tm, k), lambda bb, i, j: (bb, i, 0))

    xspec = pl.BlockSpec((None, tm, tn), lambda bb, i, j: (bb, i, j))
    return pl.pallas_call(
        functools.partial(_outproj_kernel, ka, kb), grid=grid,
        in_specs=[aspec(ka), aspec(kb), aspec(kc),
                  pl.BlockSpec((None, K, tn), lambda bb, i, j: (layer, 0, j)), xspec,
                  pl.BlockSpec((None, 1, tn), lambda bb, i, j: (bb, 0, j))],
        out_specs=xspec, out_shape=jax.ShapeDtypeStruct((B, L, D), F32),
        compiler_params=_cparams(("parallel", "parallel", "arbitrary")), name="outproj",
    )(a, b, c, w, x, gate)


def _ffn_kernel(xs_ref, wg_ref, wu_ref, wd_ref, g_ref, o_ref):
    xs = xs_ref[...]
    a = _dot(xs, wg_ref[...])
    u = _dot(xs, wu_ref[...])
    hmid = (a * jax.nn.sigmoid(a) * u).astype(BF16)
    o_ref[...] = (_dot(hmid, wd_ref[...]) * g_ref[...]).astype(o_ref.dtype)


def _experts(xs, wg, wu, wd, layer, g):
    B, E, C, D = xs.shape
    Fd = wg.shape[3]
    tc = _pick(C, 512, 8)
    grid = (E, B, C // tc)
    return pl.pallas_call(
        _ffn_kernel, grid=grid,
        in_specs=[pl.BlockSpec((None, None, tc, D), lambda e, b, i: (b, e, i, 0)),
                  pl.BlockSpec((None, None, D, Fd), lambda e, b, i: (layer, e, 0, 0)),
                  pl.BlockSpec((None, None, D, Fd), lambda e, b, i: (layer, e, 0, 0)),
                  pl.BlockSpec((None, None, Fd, D), lambda e, b, i: (layer, e, 0, 0)),
                  pl.BlockSpec((None, None, tc, 1), lambda e, b, i: (b, e, i, 0))],
        out_specs=pl.BlockSpec((None, None, tc, D), lambda e, b, i: (b, e, i, 0)),
        out_shape=jax.ShapeDtypeStruct((B, E, C, D), BF16),
        compiler_params=_cparams(("parallel", "parallel", "arbitrary")), name="experts",
    )(xs, wg, wu, wd, g)


MOE_TT = 256
MOE_RB = 256


def _combine_kernel(tt, tile_ref, blk_ref, flag_ref, tok_ref, y_ref, x_ref, g_ref, o_ref, acc_sc):
    b, j = pl.program_id(0), pl.program_id(1)
    fl = flag_ref[b, j]

    @pl.when((fl & 1) != 0)
    def _():
        acc_sc[...] = jnp.zeros(acc_sc.shape, F32)

    @pl.when((fl & 4) != 0)
    def _():
        rb = y_ref.shape[0]
        toks = tile_ref[b, j] * tt + lax.broadcasted_iota(jnp.int32, (tt, rb), 0)
        sel = jnp.where(tok_ref[...] == toks, 1.0, 0.0).astype(BF16)
        acc_sc[...] += _dot(sel, y_ref[...])

    @pl.when((fl & 2) != 0)
    def _():
        o_ref[...] = x_ref[...] + g_ref[...] * acc_sc[...]


def _combine_plan(tok_sorted, L, tt, rb):
    B, R = tok_sorted.shape
    ntile, nblk = L // tt, R // rb
    bounds = jnp.arange(ntile + 1, dtype=jnp.int32) * tt
    cut = jnp.sum(tok_sorted[:, None, :] < bounds[None, :, None], axis=-1, dtype=jnp.int32)
    lo, hi = cut[:, :-1], cut[:, 1:]
    kfirst = jnp.minimum(lo // rb, nblk - 1)
    klast = jnp.maximum(kfirst, (hi - 1) // rb)
    nb = klast - kfirst + 1
    ends = jnp.cumsum(nb, axis=1)
    starts = ends - nb
    ns = ntile + nblk
    j = jnp.arange(ns, dtype=jnp.int32)
    tile = jnp.sum(starts[:, None, :] <= j[None, :, None], axis=-1, dtype=jnp.int32) - 1
    tile = jnp.clip(tile, 0, ntile - 1)
    off = j[None, :] - jnp.take_along_axis(starts, tile, axis=1)
    nb_t = jnp.take_along_axis(nb, tile, axis=1)
    valid = j[None, :] < ends[:, -1:]
    blk = jnp.clip(jnp.take_along_axis(kfirst, tile, axis=1) + off, 0, nblk - 1)
    flags = (valid & (off == 0)) * 1 + (valid & (off == nb_t - 1)) * 2 + valid * 4
    return tile, blk.astype(jnp.int32), flags.astype(jnp.int32)


def _combine(y_sorted, tok_sorted, xres, gate):
    B, L, D = xres.shape
    R = y_sorted.shape[1]
    tt = _pick(L, MOE_TT, 8)
    rb = _pick(R, MOE_RB)
    tile, blk, flags = _combine_plan(tok_sorted, L, tt, rb)
    ns = tile.shape[1]
    tok4 = tok_sorted.reshape(B, R // rb, 1, rb)
    grid_spec = pltpu.PrefetchScalarGridSpec(
        num_scalar_prefetch=3, grid=(B, ns),
        in_specs=[
            pl.BlockSpec((None, None, 1, rb), lambda b, j, t, k, f: (b, k[b, j], 0, 0)),
            pl.BlockSpec((None, rb, D), lambda b, j, t, k, f: (b, k[b, j], 0)),
            pl.BlockSpec((None, tt, D), lambda b, j, t, k, f: (b, t[b, j], 0)),
            pl.BlockSpec((None, 1, D), lambda b, j, t, k, f: (b, 0, 0)),
        ],
        out_specs=pl.BlockSpec((None, tt, D), lambda b, j, t, k, f: (b, t[b, j], 0)),
        scratch_shapes=[pltpu.VMEM((tt, D), F32)])
    return pl.pallas_call(
        functools.partial(_combine_kernel, tt), grid_spec=grid_spec,
        out_shape=jax.ShapeDtypeStruct((B, L, D), F32),
        compiler_params=_cparams(("parallel", "arbitrary")), name="moe_combine",
    )(tile, blk, flags, tok4, y_sorted, xres, gate)


def _ec_moe(xres, gate, scale, shift, w_router, wg, wu, wd, layer):
    B, L, D = xres.shape
    E = w_router.shape[1]
    cap = max(1, EC_CAPACITY * L // E)
    h2, logits = _norm_mod(xres, scale, shift, w_router)
    aff = jax.nn.softmax(logits[..., :E], axis=-1)
    g, idx = lax.top_k(jnp.swapaxes(aff, 1, 2), cap)
    bidx = jnp.arange(B)[:, None, None]
    xs = h2[bidx, idx]
    y = _experts(xs, wg, wu, wd, layer, g[..., None]).reshape(B, E * cap, D)
    flat_tok = idx.reshape(B, E * cap).astype(jnp.int32)
    order = jnp.argsort(flat_tok, axis=1)
    tok_sorted = jnp.take_along_axis(flat_tok, order, axis=1)
    y_sorted = y[jnp.arange(B)[:, None], order]
    return _combine(y_sorted, tok_sorted, xres, gate)


def _ada(cvec, down, up, bias, n, D):
    hi = lax.Precision.HIGHEST
    z = jnp.dot(jax.nn.silu(cvec), down, precision=hi)
    m = jnp.dot(z, up[:, :n * D], precision=hi) + bias[:n * D]
    return m.reshape(cvec.shape[:-1] + (n, D))


def _rope_tables(T):
    nf = DIFF_DIM // 4
    inv = ROPE_BASE ** (-jnp.arange(nf, dtype=F32) / nf)
    t_idx = jnp.arange(T)
    ang_r = (t_idx // GRID_W).astype(F32)[:, None] * inv
    ang_c = (t_idx % GRID_W).astype(F32)[:, None] * inv
    cr, sr, cc, sc = jnp.cos(ang_r), jnp.sin(ang_r), jnp.cos(ang_c), jnp.sin(ang_c)
    cos = jnp.concatenate([cr, cr, cc, cc] * 2, axis=-1)
    sin = jnp.concatenate([-sr, sr, -sc, sc] * 2, axis=-1)
    return cos, sin


def kernel(x, c, ctx, c_ctx, ada_down, ada_up, ada_bias, w_in, conv_dw, conv_db, conv_ln_g,
           conv_ln_b, conv_pw, na_q_gain, na_k_gain, na_rpb, diff_q_gain, diff_k_gain, diff_lam,
           diff_out_gain, w_out, w_router, w_gate, w_up, w_down):
    B, T, D = x.shape
    Lc = ctx.shape[1]
    depth = w_in.shape[0]
    na_w = 3 * D // 8
    df_w = 3 * D // 8
    cch = D // 4
    o_kna, o_vna, o_kdf, o_vdf = 0, na_w, 2 * na_w, 2 * na_w + df_w
    o_qna = 2 * na_w + 2 * df_w
    o_qdf = o_qna + na_w
    o_cv = o_qdf + df_w
    o_cg = o_cv + cch
    cos, sin = _rope_tables(T)
    na_bias = _na_bias(na_rpb.astype(F32) * LOG2E)
    w_in_b, w_out_b, pw_b = w_in.astype(BF16), w_out.astype(BF16), conv_pw.astype(BF16)
    wg, wu, wd = w_gate.astype(BF16), w_up.astype(BF16), w_down.astype(BF16)

    for l in range(depth):
        update_ctx = l < depth - 1
        lam_init = 0.8 - 0.6 * math.exp(-0.3 * l)
        lv = diff_lam[l].astype(F32)
        lam = jnp.exp(jnp.sum(lv[0] * lv[1])) - jnp.exp(jnp.sum(lv[2] * lv[3])) + lam_init
        post = 1.0 - lam_init
        m = _ada(c, ada_down[l], ada_up[l], ada_bias[l], N_MOD, D)
        mc = _ada(c_ctx, ada_down[l], ada_up[l], ada_bias[l], N_MOD if update_ctx else 2, D)
        mcb = jnp.broadcast_to(mc[None], (B,) + mc.shape)
        wl = (w_in_b, l)
        wo = (w_out_b, l)
        pw = pw_b[l]
        qg_df = diff_q_gain[l].reshape(-1) * (DIFF_DIM ** -0.5 * LOG2E)
        qg_na = na_q_gain[l] * (HEAD_DIM ** -0.5 * LOG2E)
        kg_df = diff_k_gain[l].reshape(-1)

        hc = _norm_mod(ctx, mcb[:, 1:2], mcb[:, 0:1]).reshape(B * Lc, D)
        kna_c = _proj(hc, wl, o_kna, na_w, "rms128", gain=na_k_gain[l]).reshape(B, Lc, na_w)
        vna_c = _proj(hc, wl, o_vna, na_w, "plain").reshape(B, Lc, na_w)
        kdf_c = _proj(hc, wl, o_kdf, df_w, "rms64", gain=kg_df).reshape(B, Lc, df_w)
        vdf_c = _proj(hc, wl, o_vdf, df_w, "plain").reshape(B, Lc, df_w)

        h = _norm_mod(x, m[:, 1:2], m[:, 0:1]).reshape(B * T, D)
        k_na = _proj(h, wl, o_kna, na_w, "rms128", gain=na_k_gain[l]).reshape(B, T, na_w)
        v_na = _proj(h, wl, o_vna, na_w, "plain").reshape(B, T, na_w)
        k_df = _proj(h, wl, o_kdf, df_w, "rms64", gain=kg_df, cos=cos, sin=sin).reshape(B, T, df_w)
        v_df = _proj(h, wl, o_vdf, df_w, "plain").reshape(B, T, df_w)
        q_na = _proj(h, wl, o_qna, na_w, "rms128", gain=qg_na).reshape(B, T, na_w)
        q_df = _proj(h, wl, o_qdf, df_w, "rms64", gain=qg_df, cos=cos, sin=sin).reshape(B, T, df_w)
        u = _proj(h, wl, o_cv, cch, "glu", col0_b=o_cg).reshape(B, T, cch)

        o_na = _na_attn(q_na, k_na, v_na, kna_c, vna_c, na_bias[l])
        o_df = _diff_attn(q_df, kdf_c, vdf_c, k_df, v_df, lam, diff_out_gain[l], post)
        o_cv_ = _conv(u, conv_dw[l], conv_db[l], conv_ln_g[l], conv_ln_b[l], pw)
        x = _outproj(o_cv_, o_na, o_df, wo, x, m[:, 2:3])
        x = _ec_moe(x, m[:, 5:6], m[:, 4:5], m[:, 3:4], w_router[l], wg, wu, wd, l)

        if update_ctx:
            qna_c = _proj(hc, wl, o_qna, na_w, "rms128", gain=qg_na).reshape(B, Lc, na_w)
            qdf_c = _proj(hc, wl, o_qdf, df_w, "rms64", gain=qg_df).reshape(B, Lc, df_w)
            u_c = _proj(hc, wl, o_cv, cch, "glu", col0_b=o_cg).reshape(B, Lc, cch)
            o_na_c = _dense_attn(qna_c, kna_c, vna_c)
            o_df_c = _diff_attn(qdf_c, kdf_c, vdf_c, None, None, lam, diff_out_gain[l], post)
            o_cv_c = _conv(u_c, conv_dw[l], conv_db[l], conv_ln_g[l], conv_ln_b[l], pw)
            ctx = _outproj(o_cv_c, o_na_c, o_df_c, wo, ctx, mcb[:, 2:3])
            ctx = _ec_moe(ctx, mcb[:, 5:6], mcb[:, 4:5], mcb[:, 3:4], w_router[l], wg, wu, wd, l)
    return x
```

```python
import functools
import math

import numpy as np
import jax
import jax.numpy as jnp
from jax import lax
from jax.experimental import pallas as pl
from jax.experimental.pallas import tpu as pltpu

GRID_W = 64
HEAD_DIM = 128
WIN_R = 8
WIN_C = 16
DIFF_DIM = HEAD_DIM // 2
ROPE_BASE = 10000.0
EC_CAPACITY = 2
N_MOD = 6
EPS = 1e-6
LOG2E = math.log2(math.e)
LANES = 128
SUBLANES = 8
NA_QROWS = 4
NA_HPS = 6
NEG = -1e30
VMEM_LIMIT = 56 * 1024 * 1024

F32 = jnp.float32
BF16 = jnp.bfloat16


def _cparams(sem):
    return pltpu.CompilerParams(dimension_semantics=sem, vmem_limit_bytes=VMEM_LIMIT)


def _pick(n, cap, mult=LANES):
    best = None
    for t in range(mult, min(n, cap) + 1, mult):
        if n % t == 0:
            best = t
    assert best is not None, (n, cap, mult)
    return best


def _dot(a, b):
    return jnp.dot(a, b, preferred_element_type=F32)


def _dot_nt(a, b):
    return lax.dot_general(a, b, (((1,), (1,)), ((), ())), preferred_element_type=F32)


def _lane_fold(op, arrays):
    tiles = [a[:, t * LANES:(t + 1) * LANES] for a in arrays for t in range(a.shape[1] // LANES)]
    return functools.reduce(op, tiles)


def _with_ones(v):
    lane = lax.broadcasted_iota(jnp.int32, v.shape, 1)
    return jnp.concatenate([v, jnp.where(lane == 0, 1.0, 0.0).astype(v.dtype)], axis=1)


def _lane_tile(x, k):
    return jnp.concatenate([x] * k, axis=1)


def _norm_kernel(x_ref, sc_ref, sh_ref, o_ref):
    x = x_ref[...]
    ms = jnp.mean(x * x, axis=-1, keepdims=True)
    h = x * lax.rsqrt(ms + EPS) * (1.0 + sc_ref[...]) + sh_ref[...]
    o_ref[...] = h.astype(o_ref.dtype)


def _norm_router_kernel(x_ref, sc_ref, sh_ref, wr_ref, o_ref, lg_ref):
    x = x_ref[...]
    ms = jnp.mean(x * x, axis=-1, keepdims=True)
    h = (x * lax.rsqrt(ms + EPS) * (1.0 + sc_ref[...]) + sh_ref[...]).astype(o_ref.dtype)
    o_ref[...] = h
    lg_ref[...] = _dot(h, wr_ref[...])


def _norm_mod(x, scale, shift, w_router=None):
    B, L, D = x.shape
    tl = _pick(L, 512, 8)
    grid = (B, L // tl)
    xspec = pl.BlockSpec((None, tl, D), lambda b, i: (b, i, 0))
    mspec = pl.BlockSpec((None, 1, D), lambda b, i: (b, 0, 0))
    if w_router is None:
        return pl.pallas_call(
            _norm_kernel, grid=grid, in_specs=[xspec, mspec, mspec], out_specs=xspec,
            out_shape=jax.ShapeDtypeStruct((B, L, D), BF16),
            compiler_params=_cparams(("parallel", "parallel")), name="norm_mod",
        )(x, scale, shift)
    E = w_router.shape[1]
    wr = jnp.zeros((D, LANES), BF16).at[:, :E].set(w_router.astype(BF16))
    return pl.pallas_call(
        _norm_router_kernel, grid=grid,
        in_specs=[xspec, mspec, mspec, pl.BlockSpec((D, LANES), lambda b, i: (0, 0))],
        out_specs=[xspec, pl.BlockSpec((None, tl, LANES), lambda b, i: (b, i, 0))],
        out_shape=[jax.ShapeDtypeStruct((B, L, D), BF16),
                   jax.ShapeDtypeStruct((B, L, LANES), F32)],
        compiler_params=_cparams(("parallel", "parallel")), name="norm_router",
    )(x, scale, shift, wr)


PROJ_RCH = 128


def _proj_kernel(kind, rope, tn, *refs):
    if kind == "glu":
        h_ref, w_ref, w2_ref, o_ref = refs
        h = h_ref[...]
        val = _dot(h, w_ref[...])
        gate = _dot(h, w2_ref[...])
        o_ref[...] = val * jax.nn.sigmoid(gate)
        return
    if kind == "plain":
        h_ref, w_ref, o_ref = refs
        o_ref[...] = _dot(h_ref[...], w_ref[...]).astype(o_ref.dtype)
        return
    if rope:
        h_ref, w_ref, g_ref, cos_ref, sin_ref, o_ref = refs
    else:
        h_ref, w_ref, g_ref, o_ref = refs
    tm = h_ref.shape[0]
    rch = min(tm, PROJ_RCH)
    for r0 in range(0, tm, rch):
        acc = _dot(h_ref[r0:r0 + rch, :], w_ref[...])
        for s in range(tn // LANES):
            t = acc[:, s * LANES:(s + 1) * LANES]
            sq = t * t
            lane = lax.broadcasted_iota(jnp.int32, t.shape, 1)
            if kind == "rms128":
                inv = lax.rsqrt(jnp.mean(sq, axis=-1, keepdims=True) + EPS)
            else:
                lo = jnp.sum(jnp.where(lane < DIFF_DIM, sq, 0.0), axis=-1, keepdims=True)
                tot = jnp.sum(sq, axis=-1, keepdims=True)
                inv = jnp.where(lane < DIFF_DIM,
                                lax.rsqrt(lo * (1.0 / DIFF_DIM) + EPS),
                                lax.rsqrt((tot - lo) * (1.0 / DIFF_DIM) + EPS))
            t = t * inv * g_ref[...]
            if rope:
                up = pltpu.roll(t, LANES - 16, axis=1)
                dn = pltpu.roll(t, 16, axis=1)
                sw = jnp.where((lane % 32) < 16, up, dn)
                t = t * cos_ref[r0:r0 + rch, :] + sw * sin_ref[r0:r0 + rch, :]
            o_ref[r0:r0 + rch, s * LANES:(s + 1) * LANES] = t.astype(o_ref.dtype)


def _proj(h, w, col0, ncols, kind, *, gain=None, cos=None, sin=None, col0_b=None):
    w, layer = w
    M, D = h.shape
    tm = _pick(M, 1024, 8)
    tn = _pick(ncols, 512)
    assert col0 % tn == 0
    grid = (M // tm, ncols // tn)
    hspec = pl.BlockSpec((tm, D), lambda i, j: (i, 0))
    c0 = col0 // tn
    wspec = pl.BlockSpec((None, D, tn), lambda i, j: (layer, 0, c0 + j))
    ospec = pl.BlockSpec((tm, tn), lambda i, j: (i, j))
    rope = cos is not None
    ins, specs = [h, w], [hspec, wspec]
    odt = BF16
    if kind == "glu":
        assert col0_b % tn == 0
        c1 = col0_b // tn
        ins.append(w)
        specs.append(pl.BlockSpec((None, D, tn), lambda i, j: (layer, 0, c1 + j)))
        odt = F32
    elif kind != "plain":
        ins.append(gain.reshape(1, LANES).astype(F32))
        specs.append(pl.BlockSpec((1, LANES), lambda i, j: (0, 0)))
        if rope:
            nt = cos.shape[0] // tm
            tspec = pl.BlockSpec((tm, LANES), lambda i, j: (i % nt, 0))
            ins += [cos, sin]
            specs += [tspec, tspec]
    return pl.pallas_call(
        functools.partial(_proj_kernel, kind, rope, tn), grid=grid,
        in_specs=specs, out_specs=ospec,
        out_shape=jax.ShapeDtypeStruct((M, ncols), odt),
        compiler_params=_cparams(("parallel", "arbitrary")), name="proj_" + kind,
    )(*ins)


def _na_bias(rpb):
    L, H = rpb.shape[:2]
    qc = np.arange(GRID_W)
    kc = np.arange(GRID_W)
    cs = np.clip(qc - WIN_C // 2, 0, GRID_W - WIN_C)
    col_ok = (kc[None, :] >= cs[:, None]) & (kc[None, :] < cs[:, None] + WIN_C)
    dc = kc[None, :] - qc[:, None] + WIN_C - 1
    onehot = (dc[None] == np.arange(2 * WIN_C - 1)[:, None, None]) & col_ok[None]
    tq = jnp.einsum("lhrc,cqk->lhqrk", rpb.astype(F32), jnp.asarray(onehot, F32),
                    precision=lax.Precision.HIGHEST)
    tq = jnp.where(jnp.asarray(col_ok)[:, None, :], tq, NEG)
    nkr = 3 * NA_QROWS

    def neg(n):
        return jnp.full((L, H, GRID_W, n, GRID_W), NEG, F32)

    tabs = []
    for typ in range(3):
        rows_q = []
        for qr in range(NA_QROWS):
            if typ == 0:
                k0, d0 = 0, WIN_R - 1 - qr
            elif typ == 1:
                k0, d0 = qr, WIN_R // 2 - 1
            else:
                k0, d0 = nkr - WIN_R, nkr - 2 * NA_QROWS - 1 - qr
            parts = [tq[:, :, :, d0:d0 + WIN_R, :]]
            if k0:
                parts.insert(0, neg(k0))
            if nkr - k0 - WIN_R:
                parts.append(neg(nkr - k0 - WIN_R))
            rows_q.append(jnp.concatenate(parts, axis=3).reshape(L, H, GRID_W, nkr * GRID_W))
        tabs.append(jnp.concatenate(rows_q, axis=2))
    return jnp.stack(tabs, axis=1)


def _na_kernel(hps, q_ref, k0_ref, k1_ref, k2_ref, v0_ref, v1_ref, v2_ref,
               kc_ref, vc_ref, b_ref, o_ref):
    nq = q_ref.shape[0]
    krefs = (k0_ref, k1_ref, k2_ref)
    vrefs = (v0_ref, v1_ref, v2_ref)
    for hh in range(hps):
        ls = slice(hh * HEAD_DIM, (hh + 1) * HEAD_DIM)
        q = q_ref[:, ls]
        s_loc = [_dot_nt(q, krefs[d][:, ls]) + b_ref[hh, :, d * nq:(d + 1) * nq] for d in range(3)]
        s_ctx = _dot_nt(q, kc_ref[:, ls])
        m = jnp.max(_lane_fold(jnp.maximum, s_loc + [s_ctx]), axis=-1, keepdims=True)
        o = _dot(jnp.exp2((s_ctx - m).astype(BF16)), _with_ones(vc_ref[:, ls]))
        for d in range(3):
            o = o + _dot(jnp.exp2((s_loc[d] - m).astype(BF16)), _with_ones(vrefs[d][:, ls]))
        o_ref[:, ls] = (o[:, :HEAD_DIM] / o[:, HEAD_DIM:HEAD_DIM + 1]).astype(o_ref.dtype)


def _na_attn(q, k, v, kc, vc, bias):
    B, T, W = q.shape
    Lc = kc.shape[1]
    H = W // HEAD_DIM
    hps = max(d for d in range(1, NA_HPS + 1) if H % d == 0)
    hw = hps * HEAD_DIM
    nq = NA_QROWS * GRID_W
    nblk = T // nq
    assert T % nq == 0 and nblk >= 4
    grid = (H // hps, B, nblk)

    def kmap(d):
        return lambda h, b, i: (b, jnp.clip(i - 1, 0, nblk - 3) + d, h)

    qspec = pl.BlockSpec((None, nq, hw), lambda h, b, i: (b, i, h))
    kspecs = [pl.BlockSpec((None, nq, hw), kmap(d)) for d in range(3)]
    cspec = pl.BlockSpec((None, Lc, hw), lambda h, b, i: (b, 0, h))
    bspec = pl.BlockSpec(
        (None, hps, nq, 3 * nq),
        lambda h, b, i: (jnp.where(i == 0, 0, jnp.where(i == nblk - 1, 2, 1)), h, 0, 0))
    return pl.pallas_call(
        functools.partial(_na_kernel, hps), grid=grid,
        in_specs=[qspec] + kspecs + kspecs + [cspec, cspec, bspec], out_specs=qspec,
        out_shape=jax.ShapeDtypeStruct((B, T, W), BF16),
        compiler_params=_cparams(("parallel", "parallel", "arbitrary")), name="na_attn",
    )(q, k, k, k, v, v, v, kc, vc, bias)


def _dense_kernel(q_ref, k_ref, v_ref, o_ref):
    s = _dot_nt(q_ref[...], k_ref[...])
    m = jnp.max(s, axis=-1, keepdims=True)
    p = jnp.exp2(s - m)
    l = jnp.sum(p, axis=-1, keepdims=True)
    o_ref[...] = (_dot(p.astype(BF16), v_ref[...]) / l).astype(o_ref.dtype)


def _dense_attn(q, k, v):
    B, L, W = q.shape
    H = W // HEAD_DIM
    spec = pl.BlockSpec((None, L, HEAD_DIM), lambda b, h: (b, 0, h))
    return pl.pallas_call(
        _dense_kernel, grid=(B, H),
        in_specs=[spec, spec, spec], out_specs=spec,
        out_shape=jax.ShapeDtypeStruct((B, L, W), BF16),
        compiler_params=_cparams(("parallel", "parallel")), name="dense_attn",
    )(q, k, v)


def _diff_kernel(tk, n_lat, post_scale, lam_ref, q_ref, kc_ref, vc_ref, *refs):
    if n_lat:
        k_ref, v_ref, g_ref, o_ref, qs_sc, m_sc, acc_sc, vcx_sc, vx_sc, sa_sc, sb_sc = refs
    else:
        g_ref, o_ref, qs_sc, m_sc, acc_sc, vcx_sc = refs
    q = q_ref[...]
    tq = q.shape[0]

    def with_ones(dst, src):
        lane = lax.broadcasted_iota(jnp.int32, src.shape, 1)
        dst[:, 0:HEAD_DIM] = src[...]
        dst[:, HEAD_DIM:] = jnp.where(lane == 0, 1.0, 0.0).astype(BF16)

    @pl.when(pl.program_id(2) == 0)
    def _():
        with_ones(vcx_sc, vc_ref)
        if n_lat:
            with_ones(vx_sc, v_ref)

    lane = lax.broadcasted_iota(jnp.int32, q.shape, 1)
    zero = jnp.zeros_like(q)
    qs_sc[0:tq, :] = jnp.where(lane < DIFF_DIM, q, zero)
    qs_sc[tq:, :] = jnp.where(lane >= DIFF_DIM, q, zero)
    m_sc[...] = jnp.full(m_sc.shape, -jnp.inf, F32)
    acc_sc[...] = jnp.zeros(acc_sc.shape, F32)

    def softmax_pv(s, vb):
        m_old = m_sc[...]
        m_new = jnp.maximum(m_old, jnp.max(s, axis=-1, keepdims=True))
        alpha = jnp.exp2(m_old - m_new)
        p = jnp.exp2((s - _lane_tile(m_new, s.shape[1] // LANES)).astype(BF16))
        acc_sc[...] = _lane_tile(alpha, 2) * acc_sc[...] + _dot(p, vb)
        m_sc[...] = m_new

    s_ctx = _dot_nt(qs_sc[...], kc_ref[...])
    if n_lat:
        sa_sc[...] = _dot_nt(qs_sc[...], k_ref[pl.ds(0, tk), :])
    softmax_pv(s_ctx, vcx_sc[...])
    if n_lat:

        def pair(c, more):
            off0 = pl.multiple_of(c * tk, tk)
            off1 = pl.multiple_of((c + 1) * tk, tk)
            sb_sc[...] = _dot_nt(qs_sc[...], k_ref[pl.ds(off1, tk), :])
            softmax_pv(sa_sc[...], vx_sc[pl.ds(off0, tk), :])
            if more:
                off2 = pl.multiple_of((c + 2) * tk, tk)
                sa_sc[...] = _dot_nt(qs_sc[...], k_ref[pl.ds(off2, tk), :])
            softmax_pv(sb_sc[...], vx_sc[pl.ds(off1, tk), :])

        def body(c2, carry):
            pair(2 * c2, True)
            return carry

        lax.fori_loop(0, n_lat // 2 - 1, body, 0)
        pair(n_lat - 2, False)
    acc = acc_sc[...]
    o = acc[:, :HEAD_DIM] / acc[:, HEAD_DIM:HEAD_DIM + 1]
    o = o[:tq] - lam_ref[0, 0] * o[tq:]
    o = o * lax.rsqrt(jnp.mean(o * o, axis=-1, keepdims=True) + EPS)
    o_ref[...] = (o * g_ref[...] * post_scale).astype(o_ref.dtype)


def _diff_attn(q, kc, vc, k, v, lam, gain, post_scale):
    B, L, W = q.shape
    Lc = kc.shape[1]
    H = W // HEAD_DIM
    tq = _pick(L, 1024, 8)
    grid = (B, H, L // tq)
    qspec = pl.BlockSpec((None, tq, HEAD_DIM), lambda b, h, i: (b, i, h))
    cspec = pl.BlockSpec((None, Lc, HEAD_DIM), lambda b, h, i: (b, 0, h))
    ins = [lam.reshape(1, 1).astype(F32), q, kc, vc]
    specs = [pl.BlockSpec(memory_space=pltpu.SMEM), qspec, cspec, cspec]
    scratch = [pltpu.VMEM((2 * tq, HEAD_DIM), BF16), pltpu.VMEM((2 * tq, LANES), F32),
               pltpu.VMEM((2 * tq, 2 * HEAD_DIM), F32), pltpu.VMEM((Lc, 2 * HEAD_DIM), BF16)]
    tk, n_lat = 0, 0
    if k is not None:
        T = k.shape[1]
        tk = _pick(T, 1024, 8)
        n_lat = T // tk
        assert n_lat % 2 == 0
        lspec = pl.BlockSpec((None, T, HEAD_DIM), lambda b, h, i: (b, 0, h))
        ins += [k, v]
        specs += [lspec, lspec]
        scratch += [pltpu.VMEM((T, 2 * HEAD_DIM), BF16),
                    pltpu.VMEM((2 * tq, tk), F32), pltpu.VMEM((2 * tq, tk), F32)]
    ins.append(gain.reshape(1, HEAD_DIM).astype(F32))
    specs.append(pl.BlockSpec((1, HEAD_DIM), lambda b, h, i: (0, 0)))
    return pl.pallas_call(
        functools.partial(_diff_kernel, tk, n_lat, post_scale), grid=grid,
        in_specs=specs, out_specs=qspec,
        out_shape=jax.ShapeDtypeStruct((B, L, W), BF16), scratch_shapes=scratch,
        compiler_params=_cparams(("parallel", "parallel", "arbitrary")), name="diff_attn",
    )(*ins)


CONV_HALO = 16
CONV_RC = 32


def _conv_kernel(cw, nt, prev_ref, cur_ref, next_ref, dw_ref, db_ref, g_ref, b_ref,
                 pw_ref, o_ref, pad_ref, y_ref):
    i = pl.program_id(1)
    tt, ch = cur_ref.shape
    half = cw // 2
    npad = tt + 2 * CONV_HALO
    pad_ref[0, 0:CONV_HALO, :] = jnp.where(i == 0, 0.0, prev_ref[...])
    pad_ref[0, CONV_HALO:CONV_HALO + tt, :] = cur_ref[...]
    pad_ref[0, CONV_HALO + tt:, :] = jnp.where(i == nt - 1, 0.0, next_ref[...])
    for b in range(1, SUBLANES):
        pad_ref[b, 0:npad - SUBLANES, :] = pad_ref[0, b:b + npad - SUBLANES, :]
    base = CONV_HALO - half
    for r0 in range(0, tt, CONV_RC):
        for c0 in range(0, ch, LANES):
            acc = jnp.zeros((CONV_RC, LANES), F32)
            for j in range(cw):
                a, b = divmod(base + j, SUBLANES)
                acc = acc + dw_ref[j:j + 1, c0:c0 + LANES] * \
                    pad_ref[b, r0 + SUBLANES * a:r0 + SUBLANES * a + CONV_RC, c0:c0 + LANES]
            y_ref[r0:r0 + CONV_RC, c0:c0 + LANES] = acc
    y = y_ref[...] + db_ref[...]
    mu = jnp.mean(y, axis=-1, keepdims=True)
    yc = y - mu
    var = jnp.mean(yc * yc, axis=-1, keepdims=True)
    z = yc * lax.rsqrt(var + EPS) * g_ref[...] + b_ref[...]
    z = z * jax.nn.sigmoid(z)
    o_ref[...] = _dot(z.astype(BF16), pw_ref[...]).astype(o_ref.dtype)


def _conv(u, dw, db, ln_g, ln_b, pw):
    B, L, C = u.shape
    cw = dw.shape[0]
    assert cw // 2 <= CONV_HALO
    tt = _pick(L, 256, CONV_RC)
    nt = L // tt
    hb = tt // CONV_HALO
    nh = L // CONV_HALO
    cur = pl.BlockSpec((None, tt, C), lambda b, i: (b, i, 0))
    prev = pl.BlockSpec((None, CONV_HALO, C), lambda b, i: (b, jnp.maximum(i * hb - 1, 0), 0))
    nxt = pl.BlockSpec((None, CONV_HALO, C), lambda b, i: (b, jnp.minimum((i + 1) * hb, nh - 1), 0))
    vec = pl.BlockSpec((1, C), lambda b, i: (0, 0))
    return pl.pallas_call(
        functools.partial(_conv_kernel, cw, nt), grid=(B, nt),
        in_specs=[prev, cur, nxt, pl.BlockSpec((cw, C), lambda b, i: (0, 0)), vec, vec, vec,
                  pl.BlockSpec((C, C), lambda b, i: (0, 0))],
        out_specs=cur, out_shape=jax.ShapeDtypeStruct((B, L, C), BF16),
        scratch_shapes=[pltpu.VMEM((SUBLANES, tt + 2 * CONV_HALO, C), F32), pltpu.VMEM((tt, C), F32)],
        compiler_params=_cparams(("parallel", "parallel")), name="conv",
    )(u, u, u, dw.astype(F32), db.reshape(1, C).astype(F32), ln_g.reshape(1, C).astype(F32),
      ln_b.reshape(1, C).astype(F32), pw)


def _outproj_kernel(ka, kb, a_ref, b_ref, c_ref, w_ref, x_ref, g_ref, o_ref):
    acc = _dot(a_ref[...], w_ref[0:ka, :])
    acc = acc + _dot(b_ref[...], w_ref[ka:ka + kb, :])
    acc = acc + _dot(c_ref[...], w_ref[ka + kb:, :])
    o_ref[...] = x_ref[...] + g_ref[...] * acc


def _outproj(a, b, c, w, x, gate):
    w, layer = w
    B, L, D = x.shape
    ka, kb, kc = a.shape[2], b.shape[2], c.shape[2]
    K = ka + kb + kc
    tm = _pick(L, 1024, 8)
    tn = _pick(D, 512)
    grid = (B, L // tm, D // tn)

    def aspec(k):
        return pl.BlockSpec((None, tm, k), lambda bb, i, j: (bb, i, 0))

    xspec = pl.BlockSpec((None, tm, tn), lambda bb, i, j: (bb, i, j))
    return pl.pallas_call(
        functools.partial(_outproj_kernel, ka, kb), grid=grid,
        in_specs=[aspec(ka), aspec(kb), aspec(kc),
                  pl.BlockSpec((None, K, tn), lambda bb, i, j: (layer, 0, j)), xspec,
                  pl.BlockSpec((None, 1, tn), lambda bb, i, j: (bb, 0, j))],
        out_specs=xspec, out_shape=jax.ShapeDtypeStruct((B, L, D), F32),
        compiler_params=_cparams(("parallel", "parallel", "arbitrary")), name="outproj",
    )(a, b, c, w, x, gate)


def _ffn_kernel(xs_ref, wg_ref, wu_ref, wd_ref, g_ref, o_ref):
    xs = xs_ref[...]
    a = _dot(xs, wg_ref[...])
    u = _dot(xs, wu_ref[...])
    hmid = (a * jax.nn.sigmoid(a) * u).astype(BF16)
    o_ref[...] = (_dot(hmid, wd_ref[...]) * g_ref[...]).astype(o_ref.dtype)


def _experts(xs, wg, wu, wd, layer, g):
    B, E, C, D = xs.shape
    Fd = wg.shape[3]
    tc = _pick(C, 512, 8)
    grid = (E, B, C // tc)
    return pl.pallas_call(
        _ffn_kernel, grid=grid,
        in_specs=[pl.BlockSpec((None, None, tc, D), lambda e, b, i: (b, e, i, 0)),
                  pl.BlockSpec((None, None, D, Fd), lambda e, b, i: (layer, e, 0, 0)),
                  pl.BlockSpec((None, None, D, Fd), lambda e, b, i: (layer, e, 0, 0)),
                  pl.BlockSpec((None, None, Fd, D), lambda e, b, i: (layer, e, 0, 0)),
                  pl.BlockSpec((None, None, tc, 1), lambda e, b, i: (b, e, i, 0))],
        out_specs=pl.BlockSpec((None, None, tc, D), lambda e, b, i: (b, e, i, 0)),
        out_shape=jax.ShapeDtypeStruct((B, E, C, D), BF16),
        compiler_params=_cparams(("parallel", "parallel", "arbitrary")), name="experts",
    )(xs, wg, wu, wd, g)


MOE_TT = 256
MOE_RB = 256


def _combine_kernel(tt, fuse_norm, tile_ref, blk_ref, flag_ref, tok_ref, y_ref, x_ref, g_ref, *refs):
    if fuse_norm:
        sc_ref, sh_ref, o_ref, h_ref, acc_sc = refs
    else:
        o_ref, acc_sc = refs
    b, j = pl.program_id(0), pl.program_id(1)
    fl = flag_ref[b, j]

    @pl.when((fl & 1) != 0)
    def _():
        acc_sc[...] = jnp.zeros(acc_sc.shape, F32)

    @pl.when((fl & 4) != 0)
    def _():
        rb = y_ref.shape[0]
        toks = tile_ref[b, j] * tt + lax.broadcasted_iota(jnp.int32, (tt, rb), 0)
        sel = jnp.where(tok_ref[...] == toks, 1.0, 0.0).astype(BF16)
        acc_sc[...] += _dot(sel, y_ref[...])

    @pl.when((fl & 2) != 0)
    def _():
        xn = x_ref[...] + g_ref[...] * acc_sc[...]
        o_ref[...] = xn
        if fuse_norm:
            ms = jnp.mean(xn * xn, axis=-1, keepdims=True)
            h_ref[...] = (xn * lax.rsqrt(ms + EPS) * (1.0 + sc_ref[...]) + sh_ref[...]).astype(h_ref.dtype)


def _combine_plan(tok_sorted, L, tt, rb):
    B, R = tok_sorted.shape
    ntile, nblk = L // tt, R // rb
    bounds = jnp.arange(ntile + 1, dtype=jnp.int32) * tt
    cut = jnp.sum(tok_sorted[:, None, :] < bounds[None, :, None], axis=-1, dtype=jnp.int32)
    lo, hi = cut[:, :-1], cut[:, 1:]
    kfirst = jnp.minimum(lo // rb, nblk - 1)
    klast = jnp.maximum(kfirst, (hi - 1) // rb)
    nb = klast - kfirst + 1
    ends = jnp.cumsum(nb, axis=1)
    starts = ends - nb
    ns = ntile + nblk
    j = jnp.arange(ns, dtype=jnp.int32)
    tile = jnp.sum(starts[:, None, :] <= j[None, :, None], axis=-1, dtype=jnp.int32) - 1
    tile = jnp.clip(tile, 0, ntile - 1)
    off = j[None, :] - jnp.take_along_axis(starts, tile, axis=1)
    nb_t = jnp.take_along_axis(nb, tile, axis=1)
    valid = j[None, :] < ends[:, -1:]
    blk = jnp.clip(jnp.take_along_axis(kfirst, tile, axis=1) + off, 0, nblk - 1)
    flags = (valid & (off == 0)) * 1 + (valid & (off == nb_t - 1)) * 2 + valid * 4
    return tile, blk.astype(jnp.int32), flags.astype(jnp.int32)


def _combine(y_sorted, tok_sorted, xres, gate, next_mod=None):
    B, L, D = xres.shape
    R = y_sorted.shape[1]
    tt = _pick(L, MOE_TT, 8)
    rb = _pick(R, MOE_RB)
    tile, blk, flags = _combine_plan(tok_sorted, L, tt, rb)
    ns = tile.shape[1]
    tok4 = tok_sorted.reshape(B, R // rb, 1, rb)
    vspec = pl.BlockSpec((None, 1, D), lambda b, j, t, k, f: (b, 0, 0))
    xspec = pl.BlockSpec((None, tt, D), lambda b, j, t, k, f: (b, t[b, j], 0))
    ins = [tok4, y_sorted, xres, gate]
    in_specs = [pl.BlockSpec((None, None, 1, rb), lambda b, j, t, k, f: (b, k[b, j], 0, 0)),
                pl.BlockSpec((None, rb, D), lambda b, j, t, k, f: (b, k[b, j], 0)), xspec, vspec]
    out_specs, out_shape = xspec, jax.ShapeDtypeStruct((B, L, D), F32)
    if next_mod is not None:
        ins += list(next_mod)
        in_specs += [vspec, vspec]
        out_specs = [xspec, xspec]
        out_shape = [out_shape, jax.ShapeDtypeStruct((B, L, D), BF16)]
    grid_spec = pltpu.PrefetchScalarGridSpec(
        num_scalar_prefetch=3, grid=(B, ns), in_specs=in_specs, out_specs=out_specs,
        scratch_shapes=[pltpu.VMEM((tt, D), F32)])
    return pl.pallas_call(
        functools.partial(_combine_kernel, tt, next_mod is not None), grid_spec=grid_spec,
        out_shape=out_shape,
        compiler_params=_cparams(("parallel", "arbitrary")), name="moe_combine",
    )(tile, blk, flags, *ins)


def _ec_moe(xres, gate, scale, shift, w_router, wg, wu, wd, layer, next_mod=None):
    B, L, D = xres.shape
    E = w_router.shape[1]
    cap = max(1, EC_CAPACITY * L // E)
    h2, logits = _norm_mod(xres, scale, shift, w_router)
    aff = jax.nn.softmax(logits[..., :E], axis=-1)
    g, idx = lax.top_k(jnp.swapaxes(aff, 1, 2), cap)
    bidx = jnp.arange(B)[:, None, None]
    xs = h2[bidx, idx]
    y = _experts(xs, wg, wu, wd, layer, g[..., None]).reshape(B, E * cap, D)
    flat_tok = idx.reshape(B, E * cap).astype(jnp.int32)
    order = jnp.argsort(flat_tok, axis=1)
    tok_sorted = jnp.take_along_axis(flat_tok, order, axis=1)
    y_sorted = y[jnp.arange(B)[:, None], order]
    return _combine(y_sorted, tok_sorted, xres, gate, next_mod)


def _ada(cvec, down, up, bias, n, D):
    hi = lax.Precision.HIGHEST
    z = jnp.dot(jax.nn.silu(cvec), down, precision=hi)
    m = jnp.dot(z, up[:, :n * D], precision=hi) + bias[:n * D]
    return m.reshape(cvec.shape[:-1] + (n, D))


def _rope_tables(T):
    nf = DIFF_DIM // 4
    inv = ROPE_BASE ** (-jnp.arange(nf, dtype=F32) / nf)
    t_idx = jnp.arange(T)
    ang_r = (t_idx // GRID_W).astype(F32)[:, None] * inv
    ang_c = (t_idx % GRID_W).astype(F32)[:, None] * inv
    cr, sr, cc, sc = jnp.cos(ang_r), jnp.sin(ang_r), jnp.cos(ang_c), jnp.sin(ang_c)
    cos = jnp.concatenate([cr, cr, cc, cc] * 2, axis=-1)
    sin = jnp.concatenate([-sr, sr, -sc, sc] * 2, axis=-1)
    return cos, sin


def kernel(x, c, ctx, c_ctx, ada_down, ada_up, ada_bias, w_in, conv_dw, conv_db, conv_ln_g,
           conv_ln_b, conv_pw, na_q_gain, na_k_gain, na_rpb, diff_q_gain, diff_k_gain, diff_lam,
           diff_out_gain, w_out, w_router, w_gate, w_up, w_down):
    B, T, D = x.shape
    Lc = ctx.shape[1]
    depth = w_in.shape[0]
    na_w = 3 * D // 8
    df_w = 3 * D // 8
    cch = D // 4
    o_kna, o_vna, o_kdf, o_vdf = 0, na_w, 2 * na_w, 2 * na_w + df_w
    o_qna = 2 * na_w + 2 * df_w
    o_qdf = o_qna + na_w
    o_cv = o_qdf + df_w
    o_cg = o_cv + cch
    cos, sin = _rope_tables(T)
    na_bias = _na_bias(na_rpb.astype(F32) * LOG2E)
    w_in_b, w_out_b, pw_b = w_in.astype(BF16), w_out.astype(BF16), conv_pw.astype(BF16)
    wg, wu, wd = w_gate.astype(BF16), w_up.astype(BF16), w_down.astype(BF16)
    mods = [_ada(c, ada_down[l], ada_up[l], ada_bias[l], N_MOD, D) for l in range(depth)]
    h = _norm_mod(x, mods[0][:, 1:2], mods[0][:, 0:1])

    for l in range(depth):
        update_ctx = l < depth - 1
        lam_init = 0.8 - 0.6 * math.exp(-0.3 * l)
        lv = diff_lam[l].astype(F32)
        lam = jnp.exp(jnp.sum(lv[0] * lv[1])) - jnp.exp(jnp.sum(lv[2] * lv[3])) + lam_init
        post = 1.0 - lam_init
        m = mods[l]
        mc = _ada(c_ctx, ada_down[l], ada_up[l], ada_bias[l], N_MOD if update_ctx else 2, D)
        mcb = jnp.broadcast_to(mc[None], (B,) + mc.shape)
        wl = (w_in_b, l)
        wo = (w_out_b, l)
        pw = pw_b[l]
        qg_df = diff_q_gain[l].reshape(-1) * (DIFF_DIM ** -0.5 * LOG2E)
        qg_na = na_q_gain[l] * (HEAD_DIM ** -0.5 * LOG2E)
        kg_df = diff_k_gain[l].reshape(-1)

        hc = _norm_mod(ctx, mcb[:, 1:2], mcb[:, 0:1]).reshape(B * Lc, D)
        kna_c = _proj(hc, wl, o_kna, na_w, "rms128", gain=na_k_gain[l]).reshape(B, Lc, na_w)
        vna_c = _proj(hc, wl, o_vna, na_w, "plain").reshape(B, Lc, na_w)
        kdf_c = _proj(hc, wl, o_kdf, df_w, "rms64", gain=kg_df).reshape(B, Lc, df_w)
        vdf_c = _proj(hc, wl, o_vdf, df_w, "plain").reshape(B, Lc, df_w)

        h = h.reshape(B * T, D)
        k_na =_proj(h, wl, o_kna, na_w, "rms128", gain=na_k_gain[l]).reshape(B, T, na_w)
        v_na = _proj(h, wl, o_vna, na_w, "plain").reshape(B, T, na_w)
        k_df = _proj(h, wl, o_kdf, df_w, "rms64", gain=kg_df, cos=cos, sin=sin).reshape(B, T, df_w)
        v_df = _proj(h, wl, o_vdf, df_w, "plain").reshape(B, T, df_w)
        q_na = _proj(h, wl, o_qna, na_w, "rms128", gain=qg_na).reshape(B, T, na_w)
        q_df = _proj(h, wl, o_qdf, df_w, "rms64", gain=qg_df, cos=cos, sin=sin).reshape(B, T, df_w)
        u = _proj(h, wl, o_cv, cch, "glu", col0_b=o_cg).reshape(B, T, cch)

        o_na = _na_attn(q_na, k_na, v_na, kna_c, vna_c, na_bias[l])
        o_df = _diff_attn(q_df, kdf_c, vdf_c, k_df, v_df, lam, diff_out_gain[l], post)
        o_cv_ = _conv(u, conv_dw[l], conv_db[l], conv_ln_g[l], conv_ln_b[l], pw)
        x = _outproj(o_cv_, o_na, o_df, wo, x, m[:, 2:3])
        if l + 1 < depth:
            x, h = _ec_moe(x, m[:, 5:6], m[:, 4:5], m[:, 3:4], w_router[l], wg, wu, wd, l,
                           next_mod=(mods[l + 1][:, 1:2], mods[l + 1][:, 0:1]))
        else:
            x = _ec_moe(x, m[:, 5:6], m[:, 4:5], m[:, 3:4], w_router[l], wg, wu, wd, l)

        if update_ctx:
            qna_c = _proj(hc, wl, o_qna, na_w, "rms128", gain=qg_na).reshape(B, Lc, na_w)
            qdf_c = _proj(hc, wl, o_qdf, df_w, "rms64", gain=qg_df).reshape(B, Lc, df_w)
            u_c = _proj(hc, wl, o_cv, cch, "glu", col0_b=o_cg).reshape(B, Lc, cch)
            o_na_c = _dense_attn(qna_c, kna_c, vna_c)
            o_df_c = _diff_attn(qdf_c, kdf_c, vdf_c, None, None, lam, diff_out_gain[l], post)
            o_cv_c = _conv(u_c, conv_dw[l], conv_db[l], conv_ln_g[l], conv_ln_b[l], pw)
            ctx = _outproj(o_cv_c, o_na_c, o_df_c, wo, ctx, mcb[:, 2:3])
            ctx = _ec_moe(ctx, mcb[:, 5:6], mcb[:, 4:5], mcb[:, 3:4], w_router[l], wg, wu, wd, l)
    return x
```

```python
import functools
import math

import numpy as np
import jax
import jax.numpy as jnp
from jax import lax
from jax.experimental import pallas as pl
from jax.experimental.pallas import tpu as pltpu

GRID_W = 64
HEAD_DIM = 128
WIN_R = 8
WIN_C = 16
DIFF_DIM = HEAD_DIM // 2
ROPE_BASE = 10000.0
EC_CAPACITY = 2
N_MOD = 6
EPS = 1e-6
LOG2E = math.log2(math.e)
LANES = 128
SUBLANES = 8
NA_QROWS = 4
NA_HPS = 6
NEG = -1e30
VMEM_LIMIT = 56 * 1024 * 1024

F32 = jnp.float32
BF16 = jnp.bfloat16


def _cparams(sem):
    return pltpu.CompilerParams(dimension_semantics=sem, vmem_limit_bytes=VMEM_LIMIT)


def _pick(n, cap, mult=LANES):
    best = None
    for t in range(mult, min(n, cap) + 1, mult):
        if n % t == 0:
            best = t
    assert best is not None, (n, cap, mult)
    return best


def _dot(a, b):
    return jnp.dot(a, b, preferred_element_type=F32)


def _dot_nt(a, b):
    return lax.dot_general(a, b, (((1,), (1,)), ((), ())), preferred_element_type=F32)


def _lane_fold(op, arrays):
    tiles = [a[:, t * LANES:(t + 1) * LANES] for a in arrays for t in range(a.shape[1] // LANES)]
    return functools.reduce(op, tiles)


def _with_ones(v):
    lane = lax.broadcasted_iota(jnp.int32, v.shape, 1)
    return jnp.concatenate([v, jnp.where(lane == 0, 1.0, 0.0).astype(v.dtype)], axis=1)


def _lane_tile(x, k):
    return jnp.concatenate([x] * k, axis=1)


def _norm_kernel(x_ref, sc_ref, sh_ref, o_ref):
    x = x_ref[...]
    ms = jnp.mean(x * x, axis=-1, keepdims=True)
    h = x * lax.rsqrt(ms + EPS) * (1.0 + sc_ref[...]) + sh_ref[...]
    o_ref[...] = h.astype(o_ref.dtype)


def _norm_router_kernel(x_ref, sc_ref, sh_ref, wr_ref, o_ref, lg_ref):
    x = x_ref[...]
    ms = jnp.mean(x * x, axis=-1, keepdims=True)
    h = (x * lax.rsqrt(ms + EPS) * (1.0 + sc_ref[...]) + sh_ref[...]).astype(o_ref.dtype)
    o_ref[...] = h
    lg_ref[...] = _dot(h, wr_ref[...])


def _norm_mod(x, scale, shift, w_router=None):
    B, L, D = x.shape
    tl = _pick(L, 512, 8)
    grid = (B, L // tl)
    xspec = pl.BlockSpec((None, tl, D), lambda b, i: (b, i, 0))
    mspec = pl.BlockSpec((None, 1, D), lambda b, i: (b, 0, 0))
    if w_router is None:
        return pl.pallas_call(
            _norm_kernel, grid=grid, in_specs=[xspec, mspec, mspec], out_specs=xspec,
            out_shape=jax.ShapeDtypeStruct((B, L, D), BF16),
            compiler_params=_cparams(("parallel", "parallel")), name="norm_mod",
        )(x, scale, shift)
    E = w_router.shape[1]
    wr = jnp.zeros((D, LANES), BF16).at[:, :E].set(w_router.astype(BF16))
    return pl.pallas_call(
        _norm_router_kernel, grid=grid,
        in_specs=[xspec, mspec, mspec, pl.BlockSpec((D, LANES), lambda b, i: (0, 0))],
        out_specs=[xspec, pl.BlockSpec((None, tl, LANES), lambda b, i: (b, i, 0))],
        out_shape=[jax.ShapeDtypeStruct((B, L, D), BF16),
                   jax.ShapeDtypeStruct((B, L, LANES), F32)],
        compiler_params=_cparams(("parallel", "parallel")), name="norm_router",
    )(x, scale, shift, wr)


PROJ_RCH = 128


def _proj_kernel(kind, rope, tn, *refs):
    if kind == "glu":
        h_ref, w_ref, w2_ref, o_ref = refs
        h = h_ref[...]
        val = _dot(h, w_ref[...])
        gate = _dot(h, w2_ref[...])
        o_ref[...] = val * jax.nn.sigmoid(gate)
        return
    if kind == "plain":
        h_ref, w_ref, o_ref = refs
        o_ref[...] = _dot(h_ref[...], w_ref[...]).astype(o_ref.dtype)
        return
    if rope:
        h_ref, w_ref, g_ref, cos_ref, sin_ref, o_ref = refs
    else:
        h_ref, w_ref, g_ref, o_ref = refs
    tm = h_ref.shape[0]
    rch = min(tm, PROJ_RCH)
    for r0 in range(0, tm, rch):
        acc = _dot(h_ref[r0:r0 + rch, :], w_ref[...])
        for s in range(tn // LANES):
            t = acc[:, s * LANES:(s + 1) * LANES]
            sq = t * t
            lane = lax.broadcasted_iota(jnp.int32, t.shape, 1)
            if kind == "rms128":
                inv = lax.rsqrt(jnp.mean(sq, axis=-1, keepdims=True) + EPS)
            else:
                lo = jnp.sum(jnp.where(lane < DIFF_DIM, sq, 0.0), axis=-1, keepdims=True)
                tot = jnp.sum(sq, axis=-1, keepdims=True)
                inv = jnp.where(lane < DIFF_DIM,
                                lax.rsqrt(lo * (1.0 / DIFF_DIM) + EPS),
                                lax.rsqrt((tot - lo) * (1.0 / DIFF_DIM) + EPS))
            t = t * inv * g_ref[...]
            if rope:
                up = pltpu.roll(t, LANES - 16, axis=1)
                dn = pltpu.roll(t, 16, axis=1)
                sw = jnp.where((lane % 32) < 16, up, dn)
                t = t * cos_ref[r0:r0 + rch, :] + sw * sin_ref[r0:r0 + rch, :]
            o_ref[r0:r0 + rch, s * LANES:(s + 1) * LANES] = t.astype(o_ref.dtype)


def _proj(h, w, col0, ncols, kind, *, gain=None, cos=None, sin=None, col0_b=None):
    w, layer = w
    M, D = h.shape
    tm = _pick(M, 1024, 8)
    tn = _pick(ncols, 512)
    assert col0 % tn == 0
    grid = (M // tm, ncols // tn)
    hspec = pl.BlockSpec((tm, D), lambda i, j: (i, 0))
    c0 = col0 // tn
    wspec = pl.BlockSpec((None, D, tn), lambda i, j: (layer, 0, c0 + j))
    ospec = pl.BlockSpec((tm, tn), lambda i, j: (i, j))
    rope = cos is not None
    ins, specs = [h, w], [hspec, wspec]
    odt = BF16
    if kind == "glu":
        assert col0_b % tn == 0
        c1 = col0_b // tn
        ins.append(w)
        specs.append(pl.BlockSpec((None, D, tn), lambda i, j: (layer, 0, c1 + j)))
        odt = F32
    elif kind != "plain":
        ins.append(gain.reshape(1, LANES).astype(F32))
        specs.append(pl.BlockSpec((1, LANES), lambda i, j: (0, 0)))
        if rope:
            nt = cos.shape[0] // tm
            tspec = pl.BlockSpec((tm, LANES), lambda i, j: (i % nt, 0))
            ins += [cos, sin]
            specs += [tspec, tspec]
    return pl.pallas_call(
        functools.partial(_proj_kernel, kind, rope, tn), grid=grid,
        in_specs=specs, out_specs=ospec,
        out_shape=jax.ShapeDtypeStruct((M, ncols), odt),
        compiler_params=_cparams(("parallel", "arbitrary")), name="proj_" + kind,
    )(*ins)


def _na_bias(rpb):
    L, H = rpb.shape[:2]
    qc = np.arange(GRID_W)
    kc = np.arange(GRID_W)
    cs = np.clip(qc - WIN_C // 2, 0, GRID_W - WIN_C)
    col_ok = (kc[None, :] >= cs[:, None]) & (kc[None, :] < cs[:, None] + WIN_C)
    dc = kc[None, :] - qc[:, None] + WIN_C - 1
    onehot = (dc[None] == np.arange(2 * WIN_C - 1)[:, None, None]) & col_ok[None]
    tq = jnp.einsum("lhrc,cqk->lhqrk", rpb.astype(F32), jnp.asarray(onehot, F32),
                    precision=lax.Precision.HIGHEST)
    tq = jnp.where(jnp.asarray(col_ok)[:, None, :], tq, NEG)
    nkr = 3 * NA_QROWS

    def neg(n):
        return jnp.full((L, H, GRID_W, n, GRID_W), NEG, F32)

    tabs = []
    for typ in range(3):
        rows_q = []
        for qr in range(NA_QROWS):
            if typ == 0:
                k0, d0 = 0, WIN_R - 1 - qr
            elif typ == 1:
                k0, d0 = qr, WIN_R // 2 - 1
            else:
                k0, d0 = nkr - WIN_R, nkr - 2 * NA_QROWS - 1 - qr
            parts = [tq[:, :, :, d0:d0 + WIN_R, :]]
            if k0:
                parts.insert(0, neg(k0))
            if nkr - k0 - WIN_R:
                parts.append(neg(nkr - k0 - WIN_R))
            rows_q.append(jnp.concatenate(parts, axis=3).reshape(L, H, GRID_W, nkr * GRID_W))
        tabs.append(jnp.concatenate(rows_q, axis=2))
    return jnp.stack(tabs, axis=1)


def _na_kernel(hps, q_ref, k0_ref, k1_ref, k2_ref, v0_ref, v1_ref, v2_ref,
               kc_ref, vc_ref, b_ref, o_ref):
    nq = q_ref.shape[0]
    krefs = (k0_ref, k1_ref, k2_ref)
    vrefs = (v0_ref, v1_ref, v2_ref)
    for hh in range(hps):
        ls = slice(hh * HEAD_DIM, (hh + 1) * HEAD_DIM)
        q = q_ref[:, ls]
        s_loc = [_dot_nt(q, krefs[d][:, ls]) + b_ref[hh, :, d * nq:(d + 1) * nq] for d in range(3)]
        s_ctx = _dot_nt(q, kc_ref[:, ls])
        m = jnp.max(_lane_fold(jnp.maximum, s_loc + [s_ctx]), axis=-1, keepdims=True)
        o = _dot(jnp.exp2((s_ctx - m).astype(BF16)), _with_ones(vc_ref[:, ls]))
        for d in range(3):
            o = o + _dot(jnp.exp2((s_loc[d] - m).astype(BF16)), _with_ones(vrefs[d][:, ls]))
        o_ref[:, ls] = (o[:, :HEAD_DIM] / o[:, HEAD_DIM:HEAD_DIM + 1]).astype(o_ref.dtype)


def _na_attn(q, k, v, kc, vc, bias):
    B, T, W = q.shape
    Lc = kc.shape[1]
    H = W // HEAD_DIM
    hps = max(d for d in range(1, NA_HPS + 1) if H % d == 0)
    hw = hps * HEAD_DIM
    nq = NA_QROWS * GRID_W
    nblk = T // nq
    assert T % nq == 0 and nblk >= 4
    grid = (H // hps, B, nblk)

    def kmap(d):
        return lambda h, b, i: (b, jnp.clip(i - 1, 0, nblk - 3) + d, h)

    qspec = pl.BlockSpec((None, nq, hw), lambda h, b, i: (b, i, h))
    kspecs = [pl.BlockSpec((None, nq, hw), kmap(d)) for d in range(3)]
    cspec = pl.BlockSpec((None, Lc, hw), lambda h, b, i: (b, 0, h))
    bspec = pl.BlockSpec(
        (None, hps, nq, 3 * nq),
        lambda h, b, i: (jnp.where(i == 0, 0, jnp.where(i == nblk - 1, 2, 1)), h, 0, 0))
    return pl.pallas_call(
        functools.partial(_na_kernel, hps), grid=grid,
        in_specs=[qspec] + kspecs + kspecs + [cspec, cspec, bspec], out_specs=qspec,
        out_shape=jax.ShapeDtypeStruct((B, T, W), BF16),
        compiler_params=_cparams(("parallel", "parallel", "arbitrary")), name="na_attn",
    )(q, k, k, k, v, v, v, kc, vc, bias)


def _dense_kernel(q_ref, k_ref, v_ref, o_ref):
    s = _dot_nt(q_ref[...], k_ref[...])
    m = jnp.max(s, axis=-1, keepdims=True)
    p = jnp.exp2(s - m)
    l = jnp.sum(p, axis=-1, keepdims=True)
    o_ref[...] = (_dot(p.astype(BF16), v_ref[...]) / l).astype(o_ref.dtype)


def _dense_attn(q, k, v):
    B, L, W = q.shape
    H = W // HEAD_DIM
    spec = pl.BlockSpec((None, L, HEAD_DIM), lambda b, h: (b, 0, h))
    return pl.pallas_call(
        _dense_kernel, grid=(B, H),
        in_specs=[spec, spec, spec], out_specs=spec,
        out_shape=jax.ShapeDtypeStruct((B, L, W), BF16),
        compiler_params=_cparams(("parallel", "parallel")), name="dense_attn",
    )(q, k, v)


def _diff_kernel(tk, n_lat, post_scale, lam_ref, q_ref, kc_ref, vc_ref, *refs):
    if n_lat:
        k_ref, v_ref, g_ref, o_ref, qs_sc, m_sc, acc_sc, vcx_sc, vx_sc, sa_sc, sb_sc = refs
    else:
        g_ref, o_ref, qs_sc, m_sc, acc_sc, vcx_sc = refs
    q = q_ref[...]
    tq = q.shape[0]

    def with_ones(dst, src):
        lane = lax.broadcasted_iota(jnp.int32, src.shape, 1)
        dst[:, 0:HEAD_DIM] = src[...]
        dst[:, HEAD_DIM:] = jnp.where(lane == 0, 1.0, 0.0).astype(BF16)

    @pl.when(pl.program_id(2) == 0)
    def _():
        with_ones(vcx_sc, vc_ref)
        if n_lat:
            with_ones(vx_sc, v_ref)

    lane = lax.broadcasted_iota(jnp.int32, q.shape, 1)
    zero = jnp.zeros_like(q)
    qs_sc[0:tq, :] = jnp.where(lane < DIFF_DIM, q, zero)
    qs_sc[tq:, :] = jnp.where(lane >= DIFF_DIM, q, zero)
    m_sc[...] = jnp.full(m_sc.shape, -jnp.inf, F32)
    acc_sc[...] = jnp.zeros(acc_sc.shape, F32)

    def softmax_pv(s, vb):
        m_old = m_sc[...]
        m_new = jnp.maximum(m_old, jnp.max(s, axis=-1, keepdims=True))
        alpha = jnp.exp2(m_old - m_new)
        p = jnp.exp2((s - _lane_tile(m_new, s.shape[1] // LANES)).astype(BF16))
        acc_sc[...] = _lane_tile(alpha, 2) * acc_sc[...] + _dot(p, vb)
        m_sc[...] = m_new

    s_ctx = _dot_nt(qs_sc[...], kc_ref[...])
    if n_lat:
        sa_sc[...] = _dot_nt(qs_sc[...], k_ref[pl.ds(0, tk), :])
    softmax_pv(s_ctx, vcx_sc[...])
    if n_lat:

        def pair(c, more):
            off0 = pl.multiple_of(c * tk, tk)
            off1 = pl.multiple_of((c + 1) * tk, tk)
            sb_sc[...] = _dot_nt(qs_sc[...], k_ref[pl.ds(off1, tk), :])
            softmax_pv(sa_sc[...], vx_sc[pl.ds(off0, tk), :])
            if more:
                off2 = pl.multiple_of((c + 2) * tk, tk)
                sa_sc[...] = _dot_nt(qs_sc[...], k_ref[pl.ds(off2, tk), :])
            softmax_pv(sb_sc[...], vx_sc[pl.ds(off1, tk), :])

        def body(c2, carry):
            pair(2 * c2, True)
            return carry

        lax.fori_loop(0, n_lat // 2 - 1, body, 0)
        pair(n_lat - 2, False)
    acc = acc_sc[...]
    o = acc[:, :HEAD_DIM] / acc[:, HEAD_DIM:HEAD_DIM + 1]
    o = o[:tq] - lam_ref[0, 0] * o[tq:]
    o = o * lax.rsqrt(jnp.mean(o * o, axis=-1, keepdims=True) + EPS)
    o_ref[...] = (o * g_ref[...] * post_scale).astype(o_ref.dtype)


def _diff_attn(q, kc, vc, k, v, lam, gain, post_scale):
    B, L, W = q.shape
    Lc = kc.shape[1]
    H = W // HEAD_DIM
    tq = _pick(L, 1024, 8)
    grid = (B, H, L // tq)
    qspec = pl.BlockSpec((None, tq, HEAD_DIM), lambda b, h, i: (b, i, h))
    cspec = pl.BlockSpec((None, Lc, HEAD_DIM), lambda b, h, i: (b, 0, h))
    ins = [lam.reshape(1, 1).astype(F32), q, kc, vc]
    specs = [pl.BlockSpec(memory_space=pltpu.SMEM), qspec, cspec, cspec]
    scratch = [pltpu.VMEM((2 * tq, HEAD_DIM), BF16), pltpu.VMEM((2 * tq, LANES), F32),
               pltpu.VMEM((2 * tq, 2 * HEAD_DIM), F32), pltpu.VMEM((Lc, 2 * HEAD_DIM), BF16)]
    tk, n_lat = 0, 0
    if k is not None:
        T = k.shape[1]
        tk = _pick(T, 1024, 8)
        n_lat = T // tk
        assert n_lat % 2 == 0
        lspec = pl.BlockSpec((None, T, HEAD_DIM), lambda b, h, i: (b, 0, h))
        ins += [k, v]
        specs += [lspec, lspec]
        scratch += [pltpu.VMEM((T, 2 * HEAD_DIM), BF16),
                    pltpu.VMEM((2 * tq, tk), F32), pltpu.VMEM((2 * tq, tk), F32)]
    ins.append(gain.reshape(1, HEAD_DIM).astype(F32))
    specs.append(pl.BlockSpec((1, HEAD_DIM), lambda b, h, i: (0, 0)))
    return pl.pallas_call(
        functools.partial(_diff_kernel, tk, n_lat, post_scale), grid=grid,
        in_specs=specs, out_specs=qspec,
        out_shape=jax.ShapeDtypeStruct((B, L, W), BF16), scratch_shapes=scratch,
        compiler_params=_cparams(("parallel", "parallel", "arbitrary")), name="diff_attn",
    )(*ins)


CONV_HALO = 16
CONV_RC = 32


def _conv_kernel(cw, nt, prev_ref, cur_ref, next_ref, dw_ref, db_ref, g_ref, b_ref,
                 pw_ref, o_ref, pad_ref, y_ref):
    i = pl.program_id(1)
    tt, ch = cur_ref.shape
    half = cw // 2
    npad = tt + 2 * CONV_HALO
    pad_ref[0, 0:CONV_HALO, :] = jnp.where(i == 0, 0.0, prev_ref[...])
    pad_ref[0, CONV_HALO:CONV_HALO + tt, :] = cur_ref[...]
    pad_ref[0, CONV_HALO + tt:, :] = jnp.where(i == nt - 1, 0.0, next_ref[...])
    for b in range(1, SUBLANES):
        pad_ref[b, 0:npad - SUBLANES, :] = pad_ref[0, b:b + npad - SUBLANES, :]
    base = CONV_HALO - half
    for r0 in range(0, tt, CONV_RC):
        for c0 in range(0, ch, LANES):
            acc = jnp.zeros((CONV_RC, LANES), F32)
            for j in range(cw):
                a, b = divmod(base + j, SUBLANES)
                acc = acc + dw_ref[j:j + 1, c0:c0 + LANES] * \
                    pad_ref[b, r0 + SUBLANES * a:r0 + SUBLANES * a + CONV_RC, c0:c0 + LANES]
            y_ref[r0:r0 + CONV_RC, c0:c0 + LANES] = acc
    y = y_ref[...] + db_ref[...]
    mu = jnp.mean(y, axis=-1, keepdims=True)
    yc = y - mu
    var = jnp.mean(yc * yc, axis=-1, keepdims=True)
    z = yc * lax.rsqrt(var + EPS) * g_ref[...] + b_ref[...]
    z = z * jax.nn.sigmoid(z)
    o_ref[...] = _dot(z.astype(BF16), pw_ref[...]).astype(o_ref.dtype)


def _conv(u, dw, db, ln_g, ln_b, pw):
    B, L, C = u.shape
    cw = dw.shape[0]
    assert cw // 2 <= CONV_HALO
    tt = _pick(L, 256, CONV_RC)
    nt = L // tt
    hb = tt // CONV_HALO
    nh = L // CONV_HALO
    cur = pl.BlockSpec((None, tt, C), lambda b, i: (b, i, 0))
    prev = pl.BlockSpec((None, CONV_HALO, C), lambda b, i: (b, jnp.maximum(i * hb - 1, 0), 0))
    nxt = pl.BlockSpec((None, CONV_HALO, C), lambda b, i: (b, jnp.minimum((i + 1) * hb, nh - 1), 0))
    vec = pl.BlockSpec((1, C), lambda b, i: (0, 0))
    return pl.pallas_call(
        functools.partial(_conv_kernel, cw, nt), grid=(B, nt),
        in_specs=[prev, cur, nxt, pl.BlockSpec((cw, C), lambda b, i: (0, 0)), vec, vec, vec,
                  pl.BlockSpec((C, C), lambda b, i: (0, 0))],
        out_specs=cur, out_shape=jax.ShapeDtypeStruct((B, L, C), BF16),
        scratch_shapes=[pltpu.VMEM((SUBLANES, tt + 2 * CONV_HALO, C), F32), pltpu.VMEM((tt, C), F32)],
        compiler_params=_cparams(("parallel", "parallel")), name="conv",
    )(u, u, u, dw.astype(F32), db.reshape(1, C).astype(F32), ln_g.reshape(1, C).astype(F32),
      ln_b.reshape(1, C).astype(F32), pw)


def _outproj_kernel(a_ref, b_ref, c_ref, w_ref, x_ref, g_ref, o_ref):
    cat = jnp.concatenate([a_ref[...], b_ref[...], c_ref[...]], axis=1)
    o_ref[...] = x_ref[...] + g_ref[...] * _dot(cat, w_ref[...])


def _outproj(a, b, c, w, x, gate):
    w, layer = w
    B, L, D = x.shape
    ka, kb, kc = a.shape[2], b.shape[2], c.shape[2]
    K = ka + kb + kc
    tm = _pick(L, 1024, 8)
    tn = _pick(D, 512)
    grid = (B, L // tm, D // tn)

    def aspec(k):
        return pl.BlockSpec((None, tm, k), lambda bb, i, j: (bb, i, 0))

    xspec = pl.BlockSpec((None, tm, tn), lambda bb, i, j: (bb, i, j))
    return pl.pallas_call(
        _outproj_kernel, grid=grid,
        in_specs=[aspec(ka), aspec(kb), aspec(kc),
                  pl.BlockSpec((None, K, tn), lambda bb, i, j: (layer, 0, j)), xspec,
                  pl.BlockSpec((None, 1, tn), lambda bb, i, j: (bb, 0, j))],
        out_specs=xspec, out_shape=jax.ShapeDtypeStruct((B, L, D), F32),
        compiler_params=_cparams(("parallel", "parallel", "arbitrary")), name="outproj",
    )(a, b, c, w, x, gate)


def _ffn_kernel(xs_ref, wg_ref, wu_ref, wd_ref, g_ref, o_ref):
    xs = xs_ref[...]
    a = _dot(xs, wg_ref[...])
    u = _dot(xs, wu_ref[...])
    hmid = (a * jax.nn.sigmoid(a) * u).astype(BF16)
    o_ref[...] = (_dot(hmid, wd_ref[...]) * g_ref[...]).astype(o_ref.dtype)


def _experts(xs, wg, wu, wd, layer, g):
    B, E, C, D = xs.shape
    Fd = wg.shape[3]
    tc = _pick(C, 512, 8)
    grid = (E, B, C // tc)
    return pl.pallas_call(
        _ffn_kernel, grid=grid,
        in_specs=[pl.BlockSpec((None, None, tc, D), lambda e, b, i: (b, e, i, 0)),
                  pl.BlockSpec((None, None, D, Fd), lambda e, b, i: (layer, e, 0, 0)),
                  pl.BlockSpec((None, None, D, Fd), lambda e, b, i: (layer, e, 0, 0)),
                  pl.BlockSpec((None, None, Fd, D), lambda e, b, i: (layer, e, 0, 0)),
                  pl.BlockSpec((None, None, tc, 1), lambda e, b, i: (b, e, i, 0))],
        out_specs=pl.BlockSpec((None, None, tc, D), lambda e, b, i: (b, e, i, 0)),
        out_shape=jax.ShapeDtypeStruct((B, E, C, D), BF16),
        compiler_params=_cparams(("parallel", "parallel", "arbitrary")), name="experts",
    )(xs, wg, wu, wd, g)


MOE_TT = 256
MOE_RB = 256


def _combine_kernel(tt, fuse_norm, tile_ref, blk_ref, flag_ref, tok_ref, y_ref, x_ref, g_ref, *refs):
    if fuse_norm:
        sc_ref, sh_ref, o_ref, h_ref = refs
    else:
        (o_ref,) = refs
    b, j = pl.program_id(0), pl.program_id(1)
    fl = flag_ref[b, j]

    @pl.when((fl & 4) != 0)
    def _():
        rb = y_ref.shape[0]
        toks = tile_ref[b, j] * tt + lax.broadcasted_iota(jnp.int32, (tt, rb), 0)
        sel = jnp.where(tok_ref[...] == toks, 1.0, 0.0).astype(BF16)
        part = g_ref[...] * _dot(sel, y_ref[...])

        @pl.when((fl & 1) != 0)
        def _():
            o_ref[...] = x_ref[...] + part

        @pl.when((fl & 1) == 0)
        def _():
            o_ref[...] += part

    if fuse_norm:
        @pl.when((fl & 2) != 0)
        def _():
            xn = o_ref[...]
            ms = jnp.mean(xn * xn, axis=-1, keepdims=True)
            h_ref[...] = (xn * lax.rsqrt(ms + EPS) * (1.0 + sc_ref[...]) + sh_ref[...]).astype(h_ref.dtype)


def _combine_plan(tok_sorted, L, tt, rb):
    B, R = tok_sorted.shape
    ntile, nblk = L // tt, R // rb
    bounds = jnp.arange(ntile + 1, dtype=jnp.int32) * tt
    cut = jnp.sum(tok_sorted[:, None, :] < bounds[None, :, None], axis=-1, dtype=jnp.int32)
    lo, hi = cut[:, :-1], cut[:, 1:]
    kfirst = jnp.minimum(lo // rb, nblk - 1)
    klast = jnp.maximum(kfirst, (hi - 1) // rb)
    nb = klast - kfirst + 1
    ends = jnp.cumsum(nb, axis=1)
    starts = ends - nb
    ns = ntile + nblk
    j = jnp.arange(ns, dtype=jnp.int32)
    tile = jnp.sum(starts[:, None, :] <= j[None, :, None], axis=-1, dtype=jnp.int32) - 1
    tile = jnp.clip(tile, 0, ntile - 1)
    off = j[None, :] - jnp.take_along_axis(starts, tile, axis=1)
    nb_t = jnp.take_along_axis(nb, tile, axis=1)
    valid = j[None, :] < ends[:, -1:]
    blk = jnp.clip(jnp.take_along_axis(kfirst, tile, axis=1) + off, 0, nblk - 1)
    flags = (valid & (off == 0)) * 1 + (valid & (off == nb_t - 1)) * 2 + valid * 4
    return tile, blk.astype(jnp.int32), flags.astype(jnp.int32)


def _combine(y_sorted, tok_sorted, xres, gate, next_mod=None):
    B, L, D = xres.shape
    R = y_sorted.shape[1]
    tt = _pick(L, MOE_TT, 8)
    rb = _pick(R, MOE_RB)
    tile, blk, flags = _combine_plan(tok_sorted, L, tt, rb)
    ns = tile.shape[1]
    tok4 = tok_sorted.reshape(B, R // rb, 1, rb)
    vspec = pl.BlockSpec((None, 1, D), lambda b, j, t, k, f: (b, 0, 0))
    xspec = pl.BlockSpec((None, tt, D), lambda b, j, t, k, f: (b, t[b, j], 0))
    ins = [tok4, y_sorted, xres, gate]
    in_specs = [pl.BlockSpec((None, None, 1, rb), lambda b, j, t, k, f: (b, k[b, j], 0, 0)),
                pl.BlockSpec((None, rb, D), lambda b, j, t, k, f: (b, k[b, j], 0)), xspec, vspec]
    out_specs, out_shape = xspec, jax.ShapeDtypeStruct((B, L, D), F32)
    if next_mod is not None:
        ins += list(next_mod)
        in_specs += [vspec, vspec]
        out_specs = [xspec, xspec]
        out_shape = [out_shape, jax.ShapeDtypeStruct((B, L, D), BF16)]
    grid_spec = pltpu.PrefetchScalarGridSpec(
        num_scalar_prefetch=3, grid=(B, ns), in_specs=in_specs, out_specs=out_specs)
    return pl.pallas_call(
        functools.partial(_combine_kernel, tt, next_mod is not None), grid_spec=grid_spec,
        out_shape=out_shape,
        compiler_params=_cparams(("parallel", "arbitrary")), name="moe_combine",
    )(tile, blk, flags, *ins)


def _ec_moe(xres, gate, scale, shift, w_router, wg, wu, wd, layer, next_mod=None):
    B, L, D = xres.shape
    E = w_router.shape[1]
    cap = max(1, EC_CAPACITY * L // E)
    h2, logits = _norm_mod(xres, scale, shift, w_router)
    aff = jax.nn.softmax(logits[..., :E], axis=-1)
    g, idx = lax.top_k(jnp.swapaxes(aff, 1, 2), cap)
    bidx = jnp.arange(B)[:, None, None]
    xs = h2[bidx, idx]
    y = _experts(xs, wg, wu, wd, layer, g[..., None]).reshape(B, E * cap, D)
    flat_tok = idx.reshape(B, E * cap).astype(jnp.int32)
    order = jnp.argsort(flat_tok, axis=1)
    tok_sorted = jnp.take_along_axis(flat_tok, order, axis=1)
    y_sorted = y[jnp.arange(B)[:, None], order]
    return _combine(y_sorted, tok_sorted, xres, gate, next_mod)


def _ada(cvec, down, up, bias, n, D):
    hi = lax.Precision.HIGHEST
    z = jnp.dot(jax.nn.silu(cvec), down, precision=hi)
    m = jnp.dot(z, up[:, :n * D], precision=hi) + bias[:n * D]
    return m.reshape(cvec.shape[:-1] + (n, D))


def _rope_tables(T):
    nf = DIFF_DIM // 4
    inv = ROPE_BASE ** (-jnp.arange(nf, dtype=F32) / nf)
    t_idx = jnp.arange(T)
    ang_r = (t_idx // GRID_W).astype(F32)[:, None] * inv
    ang_c = (t_idx % GRID_W).astype(F32)[:, None] * inv
    cr, sr, cc, sc = jnp.cos(ang_r), jnp.sin(ang_r), jnp.cos(ang_c), jnp.sin(ang_c)
    cos = jnp.concatenate([cr, cr, cc, cc] * 2, axis=-1)
    sin = jnp.concatenate([-sr, sr, -sc, sc] * 2, axis=-1)
    return cos, sin


def kernel(x, c, ctx, c_ctx, ada_down, ada_up, ada_bias, w_in, conv_dw, conv_db, conv_ln_g,
           conv_ln_b, conv_pw, na_q_gain, na_k_gain, na_rpb, diff_q_gain, diff_k_gain, diff_lam,
           diff_out_gain, w_out, w_router, w_gate, w_up, w_down):
    B, T, D = x.shape
    Lc = ctx.shape[1]
    depth = w_in.shape[0]
    na_w = 3 * D // 8
    df_w = 3 * D // 8
    cch = D // 4
    o_kna, o_vna, o_kdf, o_vdf = 0, na_w, 2 * na_w, 2 * na_w + df_w
    o_qna = 2 * na_w + 2 * df_w
    o_qdf = o_qna + na_w
    o_cv = o_qdf + df_w
    o_cg = o_cv + cch
    cos, sin = _rope_tables(T)
    na_bias = _na_bias(na_rpb.astype(F32) * LOG2E)
    w_in_b, w_out_b, pw_b = w_in.astype(BF16), w_out.astype(BF16), conv_pw.astype(BF16)
    wg, wu, wd = w_gate.astype(BF16), w_up.astype(BF16), w_down.astype(BF16)
    mods = [_ada(c, ada_down[l], ada_up[l], ada_bias[l], N_MOD, D) for l in range(depth)]
    h = _norm_mod(x, mods[0][:, 1:2], mods[0][:, 0:1])

    for l in range(depth):
        update_ctx = l < depth - 1
        lam_init = 0.8 - 0.6 * math.exp(-0.3 * l)
        lv = diff_lam[l].astype(F32)
        lam = jnp.exp(jnp.sum(lv[0] * lv[1])) - jnp.exp(jnp.sum(lv[2] * lv[3])) + lam_init
        post = 1.0 - lam_init
        m = mods[l]
        mc = _ada(c_ctx, ada_down[l], ada_up[l], ada_bias[l], N_MOD if update_ctx else 2, D)
        mcb = jnp.broadcast_to(mc[None], (B,) + mc.shape)
        wl = (w_in_b, l)
        wo = (w_out_b, l)
        pw = pw_b[l]
        qg_df = diff_q_gain[l].reshape(-1) * (DIFF_DIM ** -0.5 * LOG2E)
        qg_na = na_q_gain[l] * (HEAD_DIM ** -0.5 * LOG2E)
        kg_df = diff_k_gain[l].reshape(-1)

        hc = _norm_mod(ctx, mcb[:, 1:2], mcb[:, 0:1]).reshape(B * Lc, D)
        kna_c = _proj(hc, wl, o_kna, na_w, "rms128", gain=na_k_gain[l]).reshape(B, Lc, na_w)
        vna_c = _proj(hc, wl, o_vna, na_w, "plain").reshape(B, Lc, na_w)
        kdf_c = _proj(hc, wl, o_kdf, df_w, "rms64", gain=kg_df).reshape(B, Lc, df_w)
        vdf_c = _proj(hc, wl, o_vdf, df_w, "plain").reshape(B, Lc, df_w)

        h = h.reshape(B * T, D)
        k_na =_proj(h, wl, o_kna, na_w, "rms128", gain=na_k_gain[l]).reshape(B, T, na_w)
        v_na = _proj(h, wl, o_vna, na_w, "plain").reshape(B, T, na_w)
        k_df = _proj(h, wl, o_kdf, df_w, "rms64", gain=kg_df, cos=cos, sin=sin).reshape(B, T, df_w)
        v_df = _proj(h, wl, o_vdf, df_w, "plain").reshape(B, T, df_w)
        q_na = _proj(h, wl, o_qna, na_w, "rms128", gain=qg_na).reshape(B, T, na_w)
        q_df = _proj(h, wl, o_qdf, df_w, "rms64", gain=qg_df, cos=cos, sin=sin).reshape(B, T, df_w)
        u = _proj(h, wl, o_cv, cch, "glu", col0_b=o_cg).reshape(B, T, cch)

        o_na = _na_attn(q_na, k_na, v_na, kna_c, vna_c, na_bias[l])
        o_df = _diff_attn(q_df, kdf_c, vdf_c, k_df, v_df, lam, diff_out_gain[l], post)
        o_cv_ = _conv(u, conv_dw[l], conv_db[l], conv_ln_g[l], conv_ln_b[l], pw)
        x = _outproj(o_cv_, o_na, o_df, wo, x, m[:, 2:3])
        if l + 1 < depth:
            x, h = _ec_moe(x, m[:, 5:6], m[:, 4:5], m[:, 3:4], w_router[l], wg, wu, wd, l,
                           next_mod=(mods[l + 1][:, 1:2], mods[l + 1][:, 0:1]))
        else:
            x = _ec_moe(x, m[:, 5:6], m[:, 4:5], m[:, 3:4], w_router[l], wg, wu, wd, l)

        if update_ctx:
            qna_c = _proj(hc, wl, o_qna, na_w, "rms128", gain=qg_na).reshape(B, Lc, na_w)
            qdf_c = _proj(hc, wl, o_qdf, df_w, "rms64", gain=qg_df).reshape(B, Lc, df_w)
            u_c = _proj(hc, wl, o_cv, cch, "glu", col0_b=o_cg).reshape(B, Lc, cch)
            o_na_c = _dense_attn(qna_c, kna_c, vna_c)
            o_df_c = _diff_attn(qdf_c, kdf_c, vdf_c, None, None, lam, diff_out_gain[l], post)
            o_cv_c = _conv(u_c, conv_dw[l], conv_db[l], conv_ln_g[l], conv_ln_b[l], pw)
            ctx = _outproj(o_cv_c, o_na_c, o_df_c, wo, ctx, mcb[:, 2:3])
            ctx = _ec_moe(ctx, mcb[:, 5:6], mcb[:, 4:5], mcb[:, 3:4], w_router[l], wg, wu, wd, l)
    return x
```
